```python
import math
import jax, jax.numpy as jnp
from jax import lax
import numpy as np

D_MODEL = 1024
BATCH = 1
SEQ = 16384
DEPTH = 4

HEAD_DIM = 64
HGRN_WIDTH = D_MODEL // 4
HGRN_HEADS = HGRN_WIDTH // HEAD_DIM
HGRN_CHUNK = 64
FOURIER_WIDTH = D_MODEL // 4
FOURIER_GROUPS = FOURIER_WIDTH // HEAD_DIM
ATTN_WIDTH = D_MODEL // 2
ATTN_HEADS = ATTN_WIDTH // HEAD_DIM
MIX_WIDTH = HGRN_WIDTH + FOURIER_WIDTH + ATTN_WIDTH
DILATED_PATTERNS = ((128, 1), (512, 4), (2048, 16))
ROPE_THETA = 10000.0
D_FF = 4 * D_MODEL
RMS_EPS = 1e-6
NEG_BIG = -1e30
IN_SPLITS = (HGRN_WIDTH, HGRN_WIDTH, HGRN_WIDTH, HGRN_WIDTH, HGRN_WIDTH, FOURIER_WIDTH, ATTN_WIDTH, ATTN_WIDTH, ATTN_WIDTH)
IN_WIDTH = sum(IN_SPLITS)
SPLIT_POINTS = [sum(IN_SPLITS[:i + 1]) for i in range(len(IN_SPLITS) - 1)]

kernel_name = "hybrid_hgrn2_fnet_dilated_attn_encoder"


def rms_norm(x, w):
    xf = x.astype(jnp.float32)
    y = xf * lax.rsqrt(jnp.mean(xf * xf, axis=-1, keepdims=True) + RMS_EPS)
    return (y * w.astype(jnp.float32)).astype(x.dtype)


def hgrn2_chunk_scan(q, k, v, log_f):
    B, H, S, dk = q.shape
    dv = v.shape[-1]
    C = HGRN_CHUNK
    n = S // C

    def chunked(t):
        return jnp.moveaxis(t.reshape(B, H, n, C, t.shape[-1]), 2, 0)

    tri = jnp.tril(jnp.ones((C, C), dtype=bool))[:, :, None]

    def step(state, inp):
        qc, kc, vc, gc = inp
        b = jnp.cumsum(gc, axis=-2)
        inter = jnp.einsum('bhtk,bhkv->bhtv', qc * jnp.exp(b), state)
        diff = b[..., :, None, :] - b[..., None, :, :]
        pair = jnp.where(tri, jnp.exp(jnp.where(tri, diff, 0.0)), 0.0)
        scores = jnp.einsum('bhtk,bhsk,bhtsk->bhts', qc, kc, pair)
        intra = jnp.einsum('bhts,bhsv->bhtv', scores, vc)
        b_last = b[..., -1:, :]
        new_state = (jnp.exp(b_last[..., 0, :])[..., None] * state
                     + jnp.einsum('bhsk,bhsv->bhkv', kc * jnp.exp(b_last - b), vc))
        return new_state, inter + intra

    init = jnp.zeros((B, H, dk, dv), dtype=jnp.float32)
    _, out = lax.scan(step, init, (chunked(q), chunked(k), chunked(v), chunked(log_f)))
    return jnp.moveaxis(out, 0, 2).reshape(B, H, S, dv)


def hgrn2_mixer(q_in, i_in, zf, zb, g_in, lb):
    B, S, W = q_in.shape
    H, dh = HGRN_HEADS, HEAD_DIM
    f32 = jnp.float32
    z = jnp.stack([zf, zb[:, ::-1]], axis=1).astype(f32)
    lbb = lb.astype(f32)[None, :, None, :]
    f = lbb + (1.0 - lbb) * jax.nn.sigmoid(z)
    log_f = jnp.log(f)
    k = 1.0 - f
    q = jax.nn.silu(q_in.astype(f32))
    v = i_in.astype(f32)
    q2 = jnp.stack([q, q[:, ::-1]], axis=1)
    v2 = jnp.stack([v, v[:, ::-1]], axis=1)

    def heads(t):
        return t.reshape(B, 2, S, H, dh).transpose(0, 1, 3, 2, 4).reshape(B, 2 * H, S, dh)

    o = hgrn2_chunk_scan(heads(q2), heads(k), heads(v2), heads(log_f)).reshape(B, 2, H, S, dh)
    o = o[:, 0] + o[:, 1][:, :, ::-1]
    o = o * lax.rsqrt(jnp.mean(o * o, axis=-1, keepdims=True) + RMS_EPS)
    o = o.transpose(0, 2, 1, 3).reshape(B, S, W)
    return o * jax.nn.silu(g_in.astype(f32))


def fourier_mixer(u):
    B, S, W = u.shape
    ug = u.astype(jnp.float32).reshape(B, S, FOURIER_GROUPS, W // FOURIER_GROUPS)
    y = jnp.fft.fft2(ug, axes=(1, 3), norm='ortho').real
    return y.reshape(B, S, W)


def rotary(t, positions):
    hd = t.shape[-1]
    inv_freq = ROPE_THETA ** (-jnp.arange(0, hd, 2, dtype=jnp.float32) / hd)
    ang = positions.astype(jnp.float32)[:, None, :, None] * inv_freq
    cos, sin = jnp.cos(ang), jnp.sin(ang)
    t1, t2 = t[..., :hd // 2], t[..., hd // 2:]
    return jnp.concatenate([t1 * cos - t2 * sin, t2 * cos + t1 * sin], axis=-1)


def dilated_window_attention(q, k, v, window, dilation):
    B, H, S, hd = q.shape
    d = dilation
    R = window // (2 * dilation)
    BLK = R
    L = S // d
    nb = -(-L // BLK)
    Lp = nb * BLK

    def sub(t):
        return t.reshape(B, H, L, d, hd).transpose(0, 1, 3, 2, 4)

    qs = jnp.pad(sub(q), ((0, 0), (0, 0), (0, 0), (0, Lp - L), (0, 0))).reshape(B, H, d, nb, BLK, hd)

    def key_blocks(t):
        tp = jnp.pad(sub(t), ((0, 0), (0, 0), (0, 0), (BLK, Lp - L + BLK), (0, 0))).reshape(B, H, d, nb + 2, BLK, hd)
        return jnp.concatenate([tp[:, :, :, :-2], tp[:, :, :, 1:-1], tp[:, :, :, 2:]], axis=-2)

    k3 = key_blocks(k)
    v3 = key_blocks(v)
    qj = jnp.arange(nb)[:, None, None] * BLK + jnp.arange(BLK)[None, :, None]
    kj = (jnp.arange(nb)[:, None, None] - 1) * BLK + jnp.arange(3 * BLK)[None, None, :]
    valid = (kj >= 0) & (kj < L) & (jnp.abs(qj - kj) <= R)
    s = jnp.einsum('bhrnqc,bhrnkc->bhrnqk', qs, k3)
    s = jnp.where(valid, s, NEG_BIG)
    lse = jax.nn.logsumexp(s, axis=-1)
    p = jnp.exp(s - lse[..., None])
    o = jnp.einsum('bhrnqk,bhrnkc->bhrnqc', p, v3)
    o = o.reshape(B, H, d, Lp, hd)[:, :, :, :L].transpose(0, 1, 3, 2, 4).reshape(B, H, S, hd)
    lse = lse.reshape(B, H, d, Lp)[..., :L].transpose(0, 1, 3, 2).reshape(B, H, S)
    return o, lse


def dilated_attention_mixer(cq, ck, cv, positions):
    B, S, W = cq.shape
    f32 = jnp.float32

    def heads(t):
        return t.astype(f32).reshape(B, S, ATTN_HEADS, HEAD_DIM).transpose(0, 2, 1, 3)

    q = rotary(heads(cq), positions) * (HEAD_DIM ** -0.5)
    k = rotary(heads(ck), positions)
    v = heads(cv)
    outs = []
    lses = []
    for window, dilation in DILATED_PATTERNS:
        o_p, lse_p = dilated_window_attention(q, k, v, window, dilation)
        outs.append(o_p)
        lses.append(lse_p)
    weights = jax.nn.softmax(jnp.stack(lses, axis=0), axis=0)
    o = jnp.sum(weights[..., None] * jnp.stack(outs, axis=0), axis=0)
    return o.transpose(0, 2, 1, 3).reshape(B, S, W)


def setup_inputs(seed: int = 0) -> dict:
    key = jax.random.key(seed)
    ks = jax.random.split(key, 10)
    f32 = jnp.float32
    x = jax.random.normal(ks[0], (BATCH, SEQ, D_MODEL), f32)
    positions = jnp.broadcast_to(jnp.arange(SEQ, dtype=jnp.int32)[None, :], (BATCH, SEQ))
    attn_norm_w = 1.0 + 0.02 * jax.random.normal(ks[1], (DEPTH, D_MODEL), f32)
    w_in = jax.random.normal(ks[2], (DEPTH, D_MODEL, IN_WIDTH), f32) * D_MODEL ** -0.5
    hgrn_lower_bounds = 0.5 * jax.random.normal(ks[3], (DEPTH, 2, HGRN_WIDTH), f32)
    w_out = jax.random.normal(ks[4], (DEPTH, MIX_WIDTH, D_MODEL), f32) * MIX_WIDTH ** -0.5
    mlp_norm_w = 1.0 + 0.02 * jax.random.normal(ks[5], (DEPTH, D_MODEL), f32)
    w_up = jax.random.normal(ks[6], (DEPTH, D_MODEL, D_FF), f32) * D_MODEL ** -0.5
    w_down = jax.random.normal(ks[7], (DEPTH, D_FF, D_MODEL), f32) * D_FF ** -0.5
    final_norm_w = 1.0 + 0.02 * jax.random.normal(ks[8], (D_MODEL,), f32)
    return {"x": x, "positions": positions, "attn_norm_w": attn_norm_w, "w_in": w_in,
            "hgrn_lower_bounds": hgrn_lower_bounds, "w_out": w_out, "mlp_norm_w": mlp_norm_w,
            "w_up": w_up, "w_down": w_down, "final_norm_w": final_norm_w}


def reference(x, positions, attn_norm_w, w_in, hgrn_lower_bounds, w_out, mlp_norm_w, w_up, w_down, final_norm_w):
    p_lb = jax.nn.softmax(hgrn_lower_bounds.astype(jnp.float32), axis=0)
    lb_all = jnp.cumsum(p_lb, axis=0) - p_lb[0:1]
    for layer in range(DEPTH):
        h = rms_norm(x, attn_norm_w[layer])
        proj = h @ w_in[layer]
        a_q, a_i, a_zf, a_zb, a_g, b_u, c_q, c_k, c_v = jnp.split(proj, SPLIT_POINTS, axis=-1)
        o_a = hgrn2_mixer(a_q, a_i, a_zf, a_zb, a_g, lb_all[layer]).astype(x.dtype)
        o_b = fourier_mixer(b_u).astype(x.dtype)
        o_c = dilated_attention_mixer(c_q, c_k, c_v, positions).astype(x.dtype)
        mixed = jnp.concatenate([o_a, o_b, o_c], axis=-1)
        x = x + mixed @ w_out[layer]
        h = rms_norm(x, mlp_norm_w[layer])
        x = x + jnp.square(jax.nn.relu(h @ w_up[layer])) @ w_down[layer]
    return rms_norm(x, final_norm_w)
```

```python
import functools
import math

import numpy as np
import jax
import jax.numpy as jnp
from jax import lax
from jax.experimental import pallas as pl
from jax.experimental.pallas import tpu as pltpu

F32 = jnp.float32
BF16 = jnp.bfloat16

D_MODEL = 1024
HEAD_DIM = 64
HGRN_WIDTH = 256
FOURIER_WIDTH = 256
FOURIER_GROUPS = 4
ATTN_WIDTH = 512
ATTN_HEADS = 8
HEAD_PAIRS = ATTN_HEADS // 2
LANES = 128
IN_WIDTH = 5 * HGRN_WIDTH + FOURIER_WIDTH + 3 * ATTN_WIDTH
D_FF = 4 * D_MODEL
DILATIONS = (1, 4, 16)
WINDOW_RADIUS = 64
ROPE_THETA = 10000.0
RMS_EPS = 1e-6
NEG_BIG = -1e30

HGRN_CHUNK = 64
HGRN_BLOCK = 256
ATTN_TILE = 2048
ATTN_QBLK = 128
ROW_TILE = 512
FF_CHUNK = 1024
VMEM_LIMIT = 56 * 1024 * 1024


def _cparams(sem):
    return pltpu.CompilerParams(dimension_semantics=sem, vmem_limit_bytes=VMEM_LIMIT)


def _const_spec(shape):
    nd = len(shape)
    return pl.BlockSpec(shape, lambda *_: (0,) * nd)


def _sigmoid(x):
    return 1.0 / (1.0 + jnp.exp(-x))


def _inproj_kernel(x_ref, nw_ref, w_ref, cos_ref, sin_ref, hg_ref, fu_ref, q_ref, k_ref, v_ref):
    x = x_ref[...]
    ms = jnp.mean(x * x, axis=-1, keepdims=True)
    h = (x * lax.rsqrt(ms + RMS_EPS) * nw_ref[...]).astype(BF16)

    def proj(c0, c1):
        return jnp.dot(h, w_ref[:, c0:c1], preferred_element_type=F32)

    hg_w = 5 * HGRN_WIDTH
    for c in range(0, hg_w, HGRN_WIDTH):
        hg_ref[:, c:c + HGRN_WIDTH] = proj(c, c + HGRN_WIDTH)
    fu_ref[...] = proj(hg_w, hg_w + FOURIER_WIDTH)

    cos = cos_ref[...]
    sin = sin_ref[...]
    lane = lax.broadcasted_iota(jnp.int32, cos.shape, 1)
    low_half = (lane % HEAD_DIM) < (HEAD_DIM // 2)

    def rotary(t):
        partner = jnp.where(low_half, pltpu.roll(t, LANES - HEAD_DIM // 2, 1),
                            pltpu.roll(t, HEAD_DIM // 2, 1))
        return t * cos + partner * sin

    base = hg_w + FOURIER_WIDTH
    for hp in range(HEAD_PAIRS):
        c = base + hp * LANES
        q_ref[hp] = (rotary(proj(c, c + LANES)) * (HEAD_DIM ** -0.5)).astype(BF16)
        c += ATTN_WIDTH
        k_ref[hp] = rotary(proj(c, c + LANES)).astype(BF16)
        c += ATTN_WIDTH
        v_ref[hp] = proj(c, c + LANES).astype(BF16)


def _inproj(x, norm_w, w_in, cos_t, sin_t):
    S = x.shape[0]
    tm = ROW_TILE
    row = lambda w: pl.BlockSpec((tm, w), lambda i: (i, 0))
    hp_spec = pl.BlockSpec((HEAD_PAIRS, tm, LANES), lambda i: (0, i, 0))
    hp_shape = jax.ShapeDtypeStruct((HEAD_PAIRS, S, LANES), BF16)
    return pl.pallas_call(
        _inproj_kernel,
        grid=(S // tm,),
        in_specs=[row(D_MODEL), _const_spec((1, D_MODEL)), _const_spec((D_MODEL, IN_WIDTH)),
                  row(LANES), row(LANES)],
        out_specs=[row(5 * HGRN_WIDTH), row(FOURIER_WIDTH), hp_spec, hp_spec, hp_spec],
        out_shape=[jax.ShapeDtypeStruct((S, 5 * HGRN_WIDTH), F32),
                   jax.ShapeDtypeStruct((S, FOURIER_WIDTH), F32), hp_shape, hp_shape, hp_shape],
        compiler_params=_cparams(("parallel",)),
        name="inproj",
    )(x, norm_w, w_in, cos_t, sin_t)


def _split3(x):
    hi = x.astype(BF16)
    r = x - hi.astype(F32)
    mid = r.astype(BF16)
    lo = (r - mid.astype(F32)).astype(BF16)
    return hi, mid, lo


def _hgrn_chunk(q_in, v, z, lb, state_t, reverse):
    C = q_in.shape[0]
    W = q_in.shape[1]
    f = lb + (1.0 - lb) * _sigmoid(z)
    log_f = jnp.log(f)
    kk = 1.0 - f
    q = q_in * _sigmoid(q_in)

    r_i = lax.broadcasted_iota(jnp.int32, (C, C), 0)
    c_i = lax.broadcasted_iota(jnp.int32, (C, C), 1)
    causal = (c_i >= r_i) if reverse else (c_i <= r_i)
    ones_tri = causal.astype(BF16)
    hi, mid, lo = _split3(log_f)
    b = (jnp.dot(ones_tri, hi, preferred_element_type=F32)
         + jnp.dot(ones_tri, mid, preferred_element_type=F32)
         + jnp.dot(ones_tri, lo, preferred_element_type=F32))
    edge = 0 if reverse else C - 1
    b_edge = b[edge:edge + 1]
    b_mid = b[C // 2:C // 2 + 1]

    q_inter = (q * jnp.exp(b)).astype(BF16)
    q_loc = (q * jnp.exp(b - b_mid)).astype(BF16)
    k_loc = (kk * jnp.exp(b_mid - b)).astype(BF16)
    k_edge = (kk * jnp.exp(b_edge - b)).astype(BF16)
    v16 = v.astype(BF16)

    contract_lanes = (((1,), (1,)), ((), ()))
    inter = lax.dot_general(q_inter, state_t.astype(BF16), contract_lanes,
                            preferred_element_type=F32)

    lane = lax.broadcasted_iota(jnp.int32, (C, W), 1)
    zero16 = jnp.zeros((C, W), BF16)
    scores = []
    v_rows = []
    for h in range(W // HEAD_DIM):
        in_head = (lane // HEAD_DIM) == h
        a = lax.dot_general(jnp.where(in_head, q_loc, zero16), k_loc, contract_lanes,
                            preferred_element_type=F32)
        scores.append(jnp.where(causal, a, 0.0).astype(BF16))
        v_rows.append(jnp.where(in_head, v16, zero16))
    intra = jnp.dot(jnp.concatenate(scores, axis=1), jnp.concatenate(v_rows, axis=0),
                    preferred_element_type=F32)

    upd = lax.dot_general(v16, k_edge, (((0,), (0,)), ((), ())),
                          preferred_element_type=F32)
    s_r = lax.broadcasted_iota(jnp.int32, (W, W), 0) // HEAD_DIM
    s_c = lax.broadcasted_iota(jnp.int32, (W, W), 1) // HEAD_DIM
    new_state = state_t * jnp.exp(b_edge) + jnp.where(s_r == s_c, upd, 0.0)
    return inter + intra, new_state


def _hgrn_kernel(qf_ref, vf_ref, zf_ref, qb_ref, vb_ref, zb_ref, lb_ref, of_ref, ob_ref,
                 sf_ref, sb_ref):
    @pl.when(pl.program_id(0) == 0)
    def _():
        sf_ref[...] = jnp.zeros_like(sf_ref)
        sb_ref[...] = jnp.zeros_like(sb_ref)

    C = HGRN_CHUNK
    n = HGRN_BLOCK // C
    lb_f = lb_ref[0:1]
    lb_b = lb_ref[1:2]
    sf = sf_ref[...]
    sb = sb_ref[...]
    for j in range(n):
        rows = slice(j * C, (j + 1) * C)
        o, sf = _hgrn_chunk(qf_ref[rows], vf_ref[rows], zf_ref[rows], lb_f, sf, False)
        of_ref[rows] = o
        rows = slice((n - 1 - j) * C, (n - j) * C)
        o, sb = _hgrn_chunk(qb_ref[rows], vb_ref[rows], zb_ref[rows], lb_b, sb, True)
        ob_ref[rows] = o
    sf_ref[...] = sf
    sb_ref[...] = sb


def _hgrn(hg, lb):
    S = hg.shape[0]
    tb = HGRN_BLOCK
    n = S // tb
    W = HGRN_WIDTH
    fwd = lambda col: pl.BlockSpec((tb, W), lambda i: (i, col))
    bwd = lambda col: pl.BlockSpec((tb, W), lambda i: (n - 1 - i, col))
    return pl.pallas_call(
        _hgrn_kernel,
        grid=(n,),
        in_specs=[fwd(0), fwd(1), fwd(2), bwd(0), bwd(1), bwd(3), _const_spec((2, W))],
        out_specs=[fwd(0), bwd(0)],
        out_shape=[jax.ShapeDtypeStruct((S, W), F32)] * 2,
        scratch_shapes=[pltpu.VMEM((W, W), F32)] * 2,
        compiler_params=_cparams(("arbitrary",)),
        name="hgrn_scan",
    )(hg, hg, hg, hg, hg, hg, lb)


def _fourier_constants(S):
    n1 = 1 << (int(math.log2(S)) // 2)
    n2 = S // n1
    cg = FOURIER_WIDTH // FOURIER_GROUPS
    j = np.arange(cg)
    ang = 2.0 * np.pi * np.outer(j, j) / cg
    eye = np.eye(FOURIER_GROUPS)
    ch = np.concatenate([np.kron(eye, np.cos(ang)), -np.kron(eye, np.sin(ang))], axis=1)
    a1 = 2.0 * np.pi * np.outer(np.arange(n1), np.arange(n1)) / n1
    fr, fi = np.cos(a1), -np.sin(a1)
    fa = np.block([[fr, -fi], [fi, fr]])
    a2 = 2.0 * np.pi * np.outer(np.arange(n2), np.arange(n2)) / n2
    fb = np.concatenate([np.cos(a2), np.sin(a2)], axis=1)
    at = 2.0 * np.pi * np.outer(np.arange(n2), np.arange(n1)) / S
    twr = np.repeat(np.cos(at)[:, :, None], LANES, axis=2)
    twi = np.repeat(-np.sin(at)[:, :, None], LANES, axis=2)
    return (n1, n2, jnp.asarray(ch, F32), jnp.asarray(fa, F32), jnp.asarray(fb, F32),
            jnp.asarray(twr, F32), jnp.asarray(twi, F32))


def _fourier_a_kernel(u_ref, ch_ref, fa_ref, twr_ref, twi_ref, z_ref, *, slabs):
    W = FOURIER_WIDTH
    ch = ch_ref[...].astype(BF16)
    fa = fa_ref[...].astype(BF16)
    for s in range(slabs):
        x = u_ref[:, s * W:(s + 1) * W].astype(BF16)
        uc = jnp.dot(x, ch, preferred_element_type=F32)
        stacked = jnp.concatenate([uc[:, :W], uc[:, W:]], axis=0).astype(BF16)
        a = jnp.dot(fa, stacked, preferred_element_type=F32)
        n1 = a.shape[0] // 2
        ar, ai = a[:n1], a[n1:]
        twr = jnp.concatenate([twr_ref[s]] * (W // LANES), axis=1)
        twi = jnp.concatenate([twi_ref[s]] * (W // LANES), axis=1)
        z_ref[0, :, s * W:(s + 1) * W] = (ar * twr - ai * twi).astype(BF16)
        z_ref[1, :, s * W:(s + 1) * W] = (ar * twi + ai * twr).astype(BF16)


def _fourier_b_kernel(z_ref, fb_ref, y_ref, *, count, scale):
    W = FOURIER_WIDTH
    fb = fb_ref[...].astype(BF16)
    for p in range(count):
        rhs = jnp.concatenate([z_ref[0, p], z_ref[1, p]], axis=0)
        y = jnp.dot(fb, rhs, preferred_element_type=F32)
        y_ref[:, p * W:(p + 1) * W] = y * scale


def _fourier(u, consts):
    S, W = u.shape
    n1, n2, ch, fa, fb, twr, twi = consts
    per = 8
    z = pl.pallas_call(
        functools.partial(_fourier_a_kernel, slabs=per),
        grid=(n2 // per,),
        in_specs=[pl.BlockSpec((n1, per * W), lambda i: (0, i)), _const_spec(ch.shape),
                  _const_spec(fa.shape), pl.BlockSpec((per, n1, LANES), lambda i: (i, 0, 0)),
                  pl.BlockSpec((per, n1, LANES), lambda i: (i, 0, 0))],
        out_specs=pl.BlockSpec((2, n1, per * W), lambda i: (0, 0, i)),
        out_shape=jax.ShapeDtypeStruct((2, n1, n2 * W), BF16),
        compiler_params=_cparams(("parallel",)),
        name="fourier_a",
    )(u.reshape(n1, n2 * W), ch, fa, twr, twi)
    cg = W // FOURIER_GROUPS
    y = pl.pallas_call(
        functools.partial(_fourier_b_kernel, count=per, scale=1.0 / math.sqrt(S * cg)),
        grid=(n1 // per,),
        in_specs=[pl.BlockSpec((2, per, n2, W), lambda i: (0, i, 0, 0)), _const_spec(fb.shape)],
        out_specs=pl.BlockSpec((n2, per * W), lambda i: (0, i)),
        out_shape=jax.ShapeDtypeStruct((n2, n1 * W), F32),
        compiler_params=_cparams(("parallel",)),
        name="fourier_b",
    )(z.reshape(2, n1, n2, W), fb)
    return y.reshape(S, W)


def _attn_kernel(q_ref, k_ref, kp_ref, kn_ref, v_ref, vp_ref, vn_ref, o_ref, lse_ref,
                 kext_ref, vext_ref, *, d):
    R = WINDOW_RADIUS
    rows = ATTN_TILE // d
    QB = ATTN_QBLK
    KW = QB + 2 * R
    t = pl.program_id(1)
    nt = pl.num_programs(1)

    kext_ref[0:R] = kp_ref[...]
    kext_ref[R:R + rows] = k_ref[...]
    kext_ref[R + rows:] = kn_ref[...]
    vext_ref[0:R] = vp_ref[...]
    vext_ref[R:R + rows] = v_ref[...]
    vext_ref[R + rows:] = vn_ref[...]

    r_i = lax.broadcasted_iota(jnp.int32, (QB, KW), 0)
    c_i = lax.broadcasted_iota(jnp.int32, (QB, KW), 1)
    band = jnp.where((c_i >= r_i) & (c_i <= r_i + 2 * R), 0.0, NEG_BIG)
    first_bias = band + jnp.where((t == 0) & (c_i < R), NEG_BIG, 0.0)
    last_bias = band + jnp.where((t == nt - 1) & (c_i >= QB + R), NEG_BIG, 0.0)
    lane = lax.broadcasted_iota(jnp.int32, (QB, LANES), 1)
    head0 = lane < HEAD_DIM
    zero16 = jnp.zeros((QB, LANES), BF16)
    contract_lanes = (((1,), (1,)), ((), ()))

    nblk = rows // QB
    for r in range(d):
        cols = slice(r * LANES, (r + 1) * LANES)
        for jb in range(nblk):
            q = q_ref[jb * QB:(jb + 1) * QB, cols]
            kwin = kext_ref[jb * QB:jb * QB + KW, cols]
            vwin = vext_ref[jb * QB:jb * QB + KW, cols]
            if nblk == 1:
                bias = first_bias + last_bias - band
            elif jb == 0:
                bias = first_bias
            elif jb == nblk - 1:
                bias = last_bias
            else:
                bias = band
            outs = []
            lses = []
            for h in range(2):
                qh = jnp.where(head0 if h == 0 else ~head0, q, zero16)
                s = lax.dot_general(qh, kwin, contract_lanes, preferred_element_type=F32) + bias
                m = jnp.max(s, axis=-1, keepdims=True)
                p = jnp.exp(s - m)
                l = jnp.sum(p, axis=-1, keepdims=True)
                pv = jnp.dot(p.astype(BF16), vwin, preferred_element_type=F32)
                outs.append(pv / l)
                lses.append(jnp.broadcast_to(m + jnp.log(l), (QB, LANES)))
            o_ref[jb * QB:(jb + 1) * QB, cols] = jnp.where(head0, outs[0], outs[1]).astype(BF16)
            lse_ref[jb * QB:(jb + 1) * QB, cols] = jnp.where(head0, lses[0], lses[1])


def _attn_pattern(q, k, v, d):
    P, S, _ = q.shape
    L = S // d
    W = d * LANES
    rows = ATTN_TILE // d
    R = WINDOW_RADIUS
    nt = S // ATTN_TILE
    halo_per_tile = rows // R
    n_halo = L // R
    view = lambda a: a.reshape(P, L, W)
    main = pl.BlockSpec((None, rows, W), lambda p, t: (p, t, 0))
    prev = pl.BlockSpec((None, R, W), lambda p, t: (p, jnp.maximum(t * halo_per_tile - 1, 0), 0))
    nxt = pl.BlockSpec((None, R, W),
                       lambda p, t: (p, jnp.minimum((t + 1) * halo_per_tile, n_halo - 1), 0))
    o, lse = pl.pallas_call(
        functools.partial(_attn_kernel, d=d),
        grid=(P, nt),
        in_specs=[main, main, prev, nxt, main, prev, nxt],
        out_specs=[main, main],
        out_shape=[jax.ShapeDtypeStruct((P, L, W), BF16), jax.ShapeDtypeStruct((P, L, W), F32)],
        scratch_shapes=[pltpu.VMEM((rows + 2 * R, W), BF16)] * 2,
        compiler_params=_cparams(("parallel", "parallel")),
        name=f"dilated_attn_d{d}",
    )(view(q), view(k), view(k), view(k), view(v), view(v), view(v))
    return o.reshape(P, S, LANES), lse.reshape(P, S, LANES)


def _rms(x, w):
    ms = jnp.mean(x * x, axis=-1, keepdims=True)
    return x * lax.rsqrt(ms + RMS_EPS) * w


def _mix_kernel(x_ref, of_ref, ob_ref, g_ref, y_ref,
                o1_ref, o2_ref, o3_ref, l1_ref, l2_ref, l3_ref, wo_ref, out_ref):
    o = of_ref[...] + ob_ref[...]
    W = o.shape[1]
    r_h = lax.broadcasted_iota(jnp.int32, (W, W), 0) // HEAD_DIM
    c_h = lax.broadcasted_iota(jnp.int32, (W, W), 1) // HEAD_DIM
    head_sum = (r_h == c_h).astype(BF16)
    sq_hi, sq_mid, _ = _split3(o * o)
    ms = (jnp.dot(sq_hi, head_sum, preferred_element_type=F32)
          + jnp.dot(sq_mid, head_sum, preferred_element_type=F32)) * (1.0 / HEAD_DIM)
    g = g_ref[...]
    o_a = o * lax.rsqrt(ms + RMS_EPS) * (g * _sigmoid(g))

    acc = x_ref[...]
    acc += jnp.dot(o_a.astype(BF16), wo_ref[0:W], preferred_element_type=F32)
    acc += jnp.dot(y_ref[...].astype(BF16), wo_ref[W:2 * W], preferred_element_type=F32)
    for hp in range(HEAD_PAIRS):
        l1, l2, l3 = l1_ref[hp], l2_ref[hp], l3_ref[hp]
        m = jnp.maximum(jnp.maximum(l1, l2), l3)
        w1, w2, w3 = jnp.exp(l1 - m), jnp.exp(l2 - m), jnp.exp(l3 - m)
        o_c = (w1 * o1_ref[hp].astype(F32) + w2 * o2_ref[hp].astype(F32)
               + w3 * o3_ref[hp].astype(F32)) / (w1 + w2 + w3)
        c = 2 * W + hp * LANES
        acc += jnp.dot(o_c.astype(BF16), wo_ref[c:c + LANES], preferred_element_type=F32)
    out_ref[...] = acc


def _mix(x, o_f, o_b, hg, y, attn, w_out):
    S = x.shape[0]
    tm = ROW_TILE
    row = lambda w, col=0: pl.BlockSpec((tm, w), lambda i: (i, col))
    hp_spec = pl.BlockSpec((HEAD_PAIRS, tm, LANES), lambda i: (0, i, 0))
    (o1, l1), (o2, l2), (o3, l3) = attn
    return pl.pallas_call(
        _mix_kernel,
        grid=(S // tm,),
        in_specs=[row(D_MODEL), row(HGRN_WIDTH), row(HGRN_WIDTH), row(HGRN_WIDTH, 4),
                  row(FOURIER_WIDTH)] + [hp_spec] * 6 + [_const_spec((D_MODEL, D_MODEL))],
        out_specs=row(D_MODEL),
        out_shape=jax.ShapeDtypeStruct((S, D_MODEL), F32),
        compiler_params=_cparams(("parallel",)),
        name="mix_outproj",
    )(x, o_f, o_b, hg, y, o1, o2, o3, l1, l2, l3, w_out)


def _mlp_kernel(x_ref, nw_ref, wu_ref, wd_ref, fw_ref, out_ref, *, final):
    acc = x_ref[...]
    h = _rms(acc, nw_ref[...]).astype(BF16)
    for c in range(0, D_FF, FF_CHUNK):
        u = jnp.dot(h, wu_ref[:, c:c + FF_CHUNK], preferred_element_type=F32)
        u = jnp.square(jnp.maximum(u, 0.0)).astype(BF16)
        acc += jnp.dot(u, wd_ref[c:c + FF_CHUNK], preferred_element_type=F32)
    out_ref[...] = _rms(acc, fw_ref[...]) if final else acc


def _mlp(x, norm_w, w_up, w_down, final_w, final):
    S = x.shape[0]
    tm = ROW_TILE
    row = pl.BlockSpec((tm, D_MODEL), lambda i: (i, 0))
    return pl.pallas_call(
        functools.partial(_mlp_kernel, final=final),
        grid=(S // tm,),
        in_specs=[row, _const_spec((1, D_MODEL)), _const_spec((D_MODEL, D_FF)),
                  _const_spec((D_FF, D_MODEL)), _const_spec((1, D_MODEL))],
        out_specs=row,
        out_shape=jax.ShapeDtypeStruct((S, D_MODEL), F32),
        compiler_params=_cparams(("parallel",)),
        name="mlp",
    )(x, norm_w, w_up, w_down, final_w)


def _rope_tables(positions):
    half = HEAD_DIM // 2
    inv_freq = ROPE_THETA ** (-jnp.arange(0, HEAD_DIM, 2, dtype=F32) / HEAD_DIM)
    ang = positions.astype(F32)[:, None] * inv_freq
    cos, sin = jnp.cos(ang), jnp.sin(ang)
    reps = LANES // HEAD_DIM
    cos_t = jnp.tile(cos, (1, 2 * reps))
    sin_t = jnp.tile(jnp.concatenate([-sin, sin], axis=1), (1, reps))
    assert cos_t.shape[1] == LANES and half * 2 == HEAD_DIM
    return cos_t, sin_t


def kernel(x, positions, attn_norm_w, w_in, hgrn_lower_bounds, w_out, mlp_norm_w, w_up, w_down,
           final_norm_w):
    B, S, D = x.shape
    depth = w_in.shape[0]
    assert B == 1 and D == D_MODEL and S % ATTN_TILE == 0 and S % (HGRN_BLOCK * 2) == 0

    p_lb = jax.nn.softmax(hgrn_lower_bounds.astype(F32), axis=0)
    lb_all = jnp.cumsum(p_lb, axis=0) - p_lb[0:1]
    cos_t, sin_t = _rope_tables(positions[0])
    fconsts = _fourier_constants(S)
    w_in16, w_out16 = w_in.astype(BF16), w_out.astype(BF16)
    w_up16, w_down16 = w_up.astype(BF16), w_down.astype(BF16)
    final_w = final_norm_w.reshape(1, D)

    xs = x[0]
    for layer in range(depth):
        hg, fu, q, k, v = _inproj(xs, attn_norm_w[layer].reshape(1, D), w_in16[layer], cos_t, sin_t)
        o_f, o_b = _hgrn(hg, lb_all[layer])
        y = _fourier(fu, fconsts)
        attn = [_attn_pattern(q, k, v, d) for d in DILATIONS]
        xs = _mix(xs, o_f, o_b, hg, y, attn, w_out16[layer])
        xs = _mlp(xs, mlp_norm_w[layer].reshape(1, D), w_up16[layer], w_down16[layer], final_w,
                  final=(layer == depth - 1))
    return xs[None]
```

```python
import functools
import math

import numpy as np
import jax
import jax.numpy as jnp
from jax import lax
from jax.experimental import pallas as pl
from jax.experimental.pallas import tpu as pltpu

F32 = jnp.float32
BF16 = jnp.bfloat16

D_MODEL = 1024
HEAD_DIM = 64
HGRN_WIDTH = 256
FOURIER_WIDTH = 256
FOURIER_GROUPS = 4
ATTN_WIDTH = 512
ATTN_HEADS = 8
HEAD_PAIRS = ATTN_HEADS // 2
LANES = 128
IN_WIDTH = 5 * HGRN_WIDTH + FOURIER_WIDTH + 3 * ATTN_WIDTH
D_FF = 4 * D_MODEL
DILATIONS = (1, 4, 16)
WINDOW_RADIUS = 64
ROPE_THETA = 10000.0
RMS_EPS = 1e-6
NEG_BIG = -1e30

HGRN_CHUNK = 64
HGRN_BLOCK = 256
ATTN_TILE = 1024
ATTN_QBLK = 128
ROW_TILE = 512
FF_CHUNK = 1024
FOURIER_PER_STEP = 8
VMEM_LIMIT = 56 * 1024 * 1024


def _cparams(sem):
    return pltpu.CompilerParams(dimension_semantics=sem, vmem_limit_bytes=VMEM_LIMIT)


def _const_spec(shape):
    nd = len(shape)
    return pl.BlockSpec(shape, lambda *_: (0,) * nd)


def _sigmoid(x):
    return 1.0 / (1.0 + jnp.exp(-x))


def _inproj_kernel(x_ref, nw_ref, w_ref, cos_ref, sin_ref, hg_ref, fu_ref, *rest):
    qkv_refs, stage_ref = rest[:-1], rest[-1]
    tm = x_ref.shape[0]
    x = x_ref[...]
    ms = jnp.mean(x * x, axis=-1, keepdims=True)
    h = (x * lax.rsqrt(ms + RMS_EPS) * nw_ref[...]).astype(BF16)

    def proj(c0, c1):
        return jnp.dot(h, w_ref[:, c0:c1], preferred_element_type=F32)

    hg_w = 5 * HGRN_WIDTH
    for c in range(0, hg_w, HGRN_WIDTH):
        hg_ref[:, c:c + HGRN_WIDTH] = proj(c, c + HGRN_WIDTH)
    fu_ref[...] = proj(hg_w, hg_w + FOURIER_WIDTH)

    cos = cos_ref[...]
    sin = sin_ref[...]
    lane = lax.broadcasted_iota(jnp.int32, cos.shape, 1)
    low_half = (lane % HEAD_DIM) < (HEAD_DIM // 2)

    def rotary(t):
        partner = jnp.where(low_half, pltpu.roll(t, LANES - HEAD_DIM // 2, 1),
                            pltpu.roll(t, HEAD_DIM // 2, 1))
        return t * cos + partner * sin

    base = hg_w + FOURIER_WIDTH
    for hp in range(HEAD_PAIRS):
        for which in range(3):
            c = base + which * ATTN_WIDTH + hp * LANES
            val = proj(c, c + LANES)
            if which == 0:
                val = rotary(val) * (HEAD_DIM ** -0.5)
            elif which == 1:
                val = rotary(val)
            stage_ref[...] = val
            for di, d in enumerate(DILATIONS):
                out_ref = qkv_refs[di * 3 + which]
                for r in range(d):
                    out_ref[hp, r] = stage_ref[pl.ds(r, tm // d, stride=d), :].astype(BF16)


def _inproj(x, norm_w, w_in, cos_t, sin_t):
    S = x.shape[0]
    tm = ROW_TILE
    row = lambda w: pl.BlockSpec((tm, w), lambda i: (i, 0))
    qkv_specs, qkv_shapes = [], []
    for d in DILATIONS:
        for _ in range(3):
            qkv_specs.append(pl.BlockSpec((HEAD_PAIRS, d, tm // d, LANES), lambda i: (0, 0, i, 0)))
            qkv_shapes.append(jax.ShapeDtypeStruct((HEAD_PAIRS, d, S // d, LANES), BF16))
    outs = pl.pallas_call(
        _inproj_kernel,
        grid=(S // tm,),
        in_specs=[row(D_MODEL), _const_spec((1, D_MODEL)), _const_spec((D_MODEL, IN_WIDTH)),
                  row(LANES), row(LANES)],
        out_specs=[row(5 * HGRN_WIDTH), row(FOURIER_WIDTH)] + qkv_specs,
        out_shape=[jax.ShapeDtypeStruct((S, 5 * HGRN_WIDTH), F32),
                   jax.ShapeDtypeStruct((S, FOURIER_WIDTH), F32)] + qkv_shapes,
        scratch_shapes=[pltpu.VMEM((tm, LANES), F32)],
        compiler_params=_cparams(("parallel",)),
        name="inproj",
    )(x, norm_w, w_in, cos_t, sin_t)
    return outs[0], outs[1], [outs[2 + 3 * i:5 + 3 * i] for i in range(len(DILATIONS))]


def _split3(x):
    hi = x.astype(BF16)
    r = x - hi.astype(F32)
    mid = r.astype(BF16)
    lo = (r - mid.astype(F32)).astype(BF16)
    return hi, mid, lo


def _hgrn_chunk(q_in, v, z, lb, state_t, reverse):
    C = q_in.shape[0]
    W = q_in.shape[1]
    f = lb + (1.0 - lb) * _sigmoid(z)
    log_f = jnp.log(f)
    kk = 1.0 - f
    q = q_in * _sigmoid(q_in)

    r_i = lax.broadcasted_iota(jnp.int32, (C, C), 0)
    c_i = lax.broadcasted_iota(jnp.int32, (C, C), 1)
    causal = (c_i >= r_i) if reverse else (c_i <= r_i)
    ones_tri = causal.astype(BF16)
    hi, mid, lo = _split3(log_f)
    b = (jnp.dot(ones_tri, hi, preferred_element_type=F32)
         + jnp.dot(ones_tri, mid, preferred_element_type=F32)
         + jnp.dot(ones_tri, lo, preferred_element_type=F32))
    edge = 0 if reverse else C - 1
    b_edge = b[edge:edge + 1]
    b_mid = b[C // 2:C // 2 + 1]

    q_inter = (q * jnp.exp(b)).astype(BF16)
    q_loc = (q * jnp.exp(b - b_mid)).astype(BF16)
    k_loc = (kk * jnp.exp(b_mid - b)).astype(BF16)
    k_edge = (kk * jnp.exp(b_edge - b)).astype(BF16)
    v16 = v.astype(BF16)

    contract_lanes = (((1,), (1,)), ((), ()))
    inter = lax.dot_general(q_inter, state_t.astype(BF16), contract_lanes,
                            preferred_element_type=F32)

    lane = lax.broadcasted_iota(jnp.int32, (C, W), 1)
    zero16 = jnp.zeros((C, W), BF16)
    scores = []
    v_rows = []
    for h in range(W // HEAD_DIM):
        in_head = (lane // HEAD_DIM) == h
        a = lax.dot_general(jnp.where(in_head, q_loc, zero16), k_loc, contract_lanes,
                            preferred_element_type=F32)
        scores.append(jnp.where(causal, a, 0.0).astype(BF16))
        v_rows.append(jnp.where(in_head, v16, zero16))
    intra = jnp.dot(jnp.concatenate(scores, axis=1), jnp.concatenate(v_rows, axis=0),
                    preferred_element_type=F32)

    upd = lax.dot_general(v16, k_edge, (((0,), (0,)), ((), ())),
                          preferred_element_type=F32)
    s_r = lax.broadcasted_iota(jnp.int32, (W, W), 0) // HEAD_DIM
    s_c = lax.broadcasted_iota(jnp.int32, (W, W), 1) // HEAD_DIM
    new_state = state_t * jnp.exp(b_edge) + jnp.where(s_r == s_c, upd, 0.0)
    return inter + intra, new_state


def _hgrn_kernel(qf_ref, vf_ref, zf_ref, qb_ref, vb_ref, zb_ref, lb_ref, of_ref, ob_ref,
                 sf_ref, sb_ref):
    @pl.when(pl.program_id(0) == 0)
    def _():
        sf_ref[...] = jnp.zeros_like(sf_ref)
        sb_ref[...] = jnp.zeros_like(sb_ref)

    C = HGRN_CHUNK
    n = HGRN_BLOCK // C
    lb_f = lb_ref[0:1]
    lb_b = lb_ref[1:2]
    sf = sf_ref[...]
    sb = sb_ref[...]
    for j in range(n):
        rows = slice(j * C, (j + 1) * C)
        o, sf = _hgrn_chunk(qf_ref[rows], vf_ref[rows], zf_ref[rows], lb_f, sf, False)
        of_ref[rows] = o
        rows = slice((n - 1 - j) * C, (n - j) * C)
        o, sb = _hgrn_chunk(qb_ref[rows], vb_ref[rows], zb_ref[rows], lb_b, sb, True)
        ob_ref[rows] = o
    sf_ref[...] = sf
    sb_ref[...] = sb


def _hgrn(hg, lb):
    S = hg.shape[0]
    tb = HGRN_BLOCK
    n = S // tb
    W = HGRN_WIDTH
    fwd = lambda col: pl.BlockSpec((tb, W), lambda i: (i, col))
    bwd = lambda col: pl.BlockSpec((tb, W), lambda i: (n - 1 - i, col))
    return pl.pallas_call(
        _hgrn_kernel,
        grid=(n,),
        in_specs=[fwd(0), fwd(1), fwd(2), bwd(0), bwd(1), bwd(3), _const_spec((2, W))],
        out_specs=[fwd(0), bwd(0)],
        out_shape=[jax.ShapeDtypeStruct((S, W), F32)] * 2,
        scratch_shapes=[pltpu.VMEM((W, W), F32)] * 2,
        compiler_params=_cparams(("arbitrary",)),
        name="hgrn_scan",
    )(hg, hg, hg, hg, hg, hg, lb)


def _fourier_constants(S):
    n1 = 1 << (int(math.log2(S)) // 2)
    n2 = S // n1
    cg = FOURIER_WIDTH // FOURIER_GROUPS
    j = np.arange(cg)
    ang = 2.0 * np.pi * np.outer(j, j) / cg
    eye = np.eye(FOURIER_GROUPS)
    ch = np.concatenate([np.kron(eye, np.cos(ang)), -np.kron(eye, np.sin(ang))], axis=1)
    a1 = 2.0 * np.pi * np.outer(np.arange(n1), np.arange(n1)) / n1
    fr, fi = np.cos(a1), -np.sin(a1)
    fa = np.block([[fr, -fi], [fi, fr]])
    a2 = 2.0 * np.pi * np.outer(np.arange(n2), np.arange(n2)) / n2
    fb = np.concatenate([np.cos(a2), np.sin(a2)], axis=1)
    at = 2.0 * np.pi * np.outer(np.arange(n2), np.arange(n1)) / S
    twr = np.repeat(np.cos(at)[:, :, None], LANES, axis=2)
    twi = np.repeat(-np.sin(at)[:, :, None], LANES, axis=2)
    return (n1, n2, jnp.asarray(ch, F32), jnp.asarray(fa, F32), jnp.asarray(fb, F32),
            jnp.asarray(twr, F32), jnp.asarray(twi, F32))


def _fourier_a_kernel(u_ref, ch_ref, fa_ref, twr_ref, twi_ref, z_ref):
    W = FOURIER_WIDTH
    ch = ch_ref[...].astype(BF16)
    fa = fa_ref[...].astype(BF16)
    for s in range(u_ref.shape[1]):
        x = u_ref[:, s, :].astype(BF16)
        uc = jnp.dot(x, ch, preferred_element_type=F32)
        stacked = jnp.concatenate([uc[:, :W], uc[:, W:]], axis=0).astype(BF16)
        a = jnp.dot(fa, stacked, preferred_element_type=F32)
        n1 = a.shape[0] // 2
        ar, ai = a[:n1], a[n1:]
        twr = jnp.concatenate([twr_ref[s]] * (W // LANES), axis=1)
        twi = jnp.concatenate([twi_ref[s]] * (W // LANES), axis=1)
        z_ref[0, :, s, :] = ar * twr - ai * twi
        z_ref[1, :, s, :] = ar * twi + ai * twr


def _fourier_b_kernel(z_ref, fb_ref, y_ref, *, scale):
    fb = fb_ref[...].astype(BF16)
    for p in range(z_ref.shape[1]):
        rhs = jnp.concatenate([z_ref[0, p], z_ref[1, p]], axis=0).astype(BF16)
        y_ref[:, p, :] = jnp.dot(fb, rhs, preferred_element_type=F32) * scale


def _fourier(u, consts):
    S, W = u.shape
    n1, n2, ch, fa, fb, twr, twi = consts
    per = FOURIER_PER_STEP
    z = pl.pallas_call(
        _fourier_a_kernel,
        grid=(n2 // per,),
        in_specs=[pl.BlockSpec((n1, per, W), lambda i: (0, i, 0)), _const_spec(ch.shape),
                  _const_spec(fa.shape), pl.BlockSpec((per, n1, LANES), lambda i: (i, 0, 0)),
                  pl.BlockSpec((per, n1, LANES), lambda i: (i, 0, 0))],
        out_specs=pl.BlockSpec((2, n1, per, W), lambda i: (0, 0, i, 0)),
        out_shape=jax.ShapeDtypeStruct((2, n1, n2, W), F32),
        compiler_params=_cparams(("parallel",)),
        name="fourier_a",
    )(u.reshape(n1, n2, W), ch, fa, twr, twi)
    cg = W // FOURIER_GROUPS
    y = pl.pallas_call(
        functools.partial(_fourier_b_kernel, scale=1.0 / math.sqrt(S * cg)),
        grid=(n1 // per,),
        in_specs=[pl.BlockSpec((2, per, n2, W), lambda i: (0, i, 0, 0)), _const_spec(fb.shape)],
        out_specs=pl.BlockSpec((n2, per, W), lambda i: (0, i, 0)),
        out_shape=jax.ShapeDtypeStruct((n2, n1, W), F32),
        compiler_params=_cparams(("parallel",)),
        name="fourier_b",
    )(z, fb)
    return y.reshape(S, W)


def _attn_kernel(q_ref, k_ref, kp_ref, kn_ref, v_ref, vp_ref, vn_ref, o_ref, lse_ref,
                 kext_ref, vext_ref):
    R = WINDOW_RADIUS
    rows = q_ref.shape[0]
    QB = ATTN_QBLK
    KW = QB + 2 * R
    t = pl.program_id(1)
    nt = pl.num_programs(1)

    kext_ref[0:R] = kp_ref[...]
    kext_ref[R:R + rows] = k_ref[...]
    kext_ref[R + rows:] = kn_ref[...]
    vext_ref[0:R] = vp_ref[...]
    vext_ref[R:R + rows] = v_ref[...]
    vext_ref[R + rows:] = vn_ref[...]

    r_i = lax.broadcasted_iota(jnp.int32, (QB, KW), 0)
    c_i = lax.broadcasted_iota(jnp.int32, (QB, KW), 1)
    band = jnp.where((c_i >= r_i) & (c_i <= r_i + 2 * R), 0.0, NEG_BIG)
    first_bias = band + jnp.where((t == 0) & (c_i < R), NEG_BIG, 0.0)
    last_bias = band + jnp.where((t == nt - 1) & (c_i >= QB + R), NEG_BIG, 0.0)
    lane = lax.broadcasted_iota(jnp.int32, (QB, LANES), 1)
    head0 = lane < HEAD_DIM
    zero16 = jnp.zeros((QB, LANES), BF16)
    contract_lanes = (((1,), (1,)), ((), ()))

    nblk = rows // QB
    for jb in range(nblk):
        q = q_ref[jb * QB:(jb + 1) * QB]
        kwin = kext_ref[jb * QB:jb * QB + KW]
        vwin = vext_ref[jb * QB:jb * QB + KW]
        bias = first_bias if jb == 0 else (last_bias if jb == nblk - 1 else band)
        outs = []
        lses = []
        for h in range(2):
            qh = jnp.where(head0 if h == 0 else ~head0, q, zero16)
            s = lax.dot_general(qh, kwin, contract_lanes, preferred_element_type=F32) + bias
            m = jnp.max(s, axis=-1, keepdims=True)
            p = jnp.exp(s - m)
            l = jnp.sum(p, axis=-1, keepdims=True)
            pv = jnp.dot(p.astype(BF16), vwin, preferred_element_type=F32)
            outs.append(pv / l)
            lses.append(jnp.broadcast_to(m + jnp.log(l), (QB, LANES)))
        o_ref[jb * QB:(jb + 1) * QB] = jnp.where(head0, outs[0], outs[1]).astype(BF16)
        lse_ref[jb * QB:(jb + 1) * QB] = jnp.where(head0, lses[0], lses[1])


def _attn_pattern(q, k, v):
    P, d, L, _ = q.shape
    rows = ATTN_TILE
    assert L % rows == 0 and rows % ATTN_QBLK == 0 and rows // ATTN_QBLK >= 2
    R = WINDOW_RADIUS
    halo_per_tile = rows // R
    n_halo = L // R
    flat = lambda a: a.reshape(P * d, L, LANES)
    main = pl.BlockSpec((None, rows, LANES), lambda b, t: (b, t, 0))
    prev = pl.BlockSpec((None, R, LANES),
                        lambda b, t: (b, jnp.maximum(t * halo_per_tile - 1, 0), 0))
    nxt = pl.BlockSpec((None, R, LANES),
                       lambda b, t: (b, jnp.minimum((t + 1) * halo_per_tile, n_halo - 1), 0))
    o, lse = pl.pallas_call(
        _attn_kernel,
        grid=(P * d, L // rows),
        in_specs=[main, main, prev, nxt, main, prev, nxt],
        out_specs=[main, main],
        out_shape=[jax.ShapeDtypeStruct((P * d, L, LANES), BF16),
                   jax.ShapeDtypeStruct((P * d, L, LANES), F32)],
        scratch_shapes=[pltpu.VMEM((rows + 2 * R, LANES), BF16)] * 2,
        compiler_params=_cparams(("parallel", "parallel")),
        name=f"dilated_attn_d{d}",
    )(flat(q), flat(k), flat(k), flat(k), flat(v), flat(v), flat(v))
    return o.reshape(P, d, L, LANES), lse.reshape(P, d, L, LANES)


def _rms(x, w):
    ms = jnp.mean(x * x, axis=-1, keepdims=True)
    return x * lax.rsqrt(ms + RMS_EPS) * w


def _mix_kernel(x_ref, of_ref, ob_ref, g_ref, y_ref, *rest):
    attn_refs, wo_ref, out_ref, stage_ref = rest[:-3], rest[-3], rest[-2], rest[-1]
    tm = x_ref.shape[0]
    o = of_ref[...] + ob_ref[...]
    W = o.shape[1]
    r_h = lax.broadcasted_iota(jnp.int32, (W, W), 0) // HEAD_DIM
    c_h = lax.broadcasted_iota(jnp.int32, (W, W), 1) // HEAD_DIM
    head_sum = (r_h == c_h).astype(BF16)
    sq_hi, sq_mid, _ = _split3(o * o)
    ms = (jnp.dot(sq_hi, head_sum, preferred_element_type=F32)
          + jnp.dot(sq_mid, head_sum, preferred_element_type=F32)) * (1.0 / HEAD_DIM)
    g = g_ref[...]
    o_a = o * lax.rsqrt(ms + RMS_EPS) * (g * _sigmoid(g))

    acc = x_ref[...]
    acc += jnp.dot(o_a.astype(BF16), wo_ref[0:W], preferred_element_type=F32)
    acc += jnp.dot(y_ref[...].astype(BF16), wo_ref[W:2 * W], preferred_element_type=F32)

    def natural(ref, hp, d):
        if d == 1:
            return ref[hp, 0].astype(F32)
        for r in range(d):
            stage_ref[pl.ds(r, tm // d, stride=d), :] = ref[hp, r].astype(F32)
        return stage_ref[...]

    for hp in range(HEAD_PAIRS):
        outs = [natural(attn_refs[2 * i], hp, d) for i, d in enumerate(DILATIONS)]
        lses = [natural(attn_refs[2 * i + 1], hp, d) for i, d in enumerate(DILATIONS)]
        m = functools.reduce(jnp.maximum, lses)
        ws = [jnp.exp(l - m) for l in lses]
        o_c = sum(w * ov for w, ov in zip(ws, outs)) / sum(ws)
        c = 2 * W + hp * LANES
        acc += jnp.dot(o_c.astype(BF16), wo_ref[c:c + LANES], preferred_element_type=F32)
    out_ref[...] = acc


def _mix(x, o_f, o_b, hg, y, attn, w_out):
    S = x.shape[0]
    tm = ROW_TILE
    row = lambda w, col=0: pl.BlockSpec((tm, w), lambda i: (i, col))
    attn_specs, attn_args = [], []
    for d, (o, lse) in zip(DILATIONS, attn):
        spec = pl.BlockSpec((HEAD_PAIRS, d, tm // d, LANES), lambda i: (0, 0, i, 0))
        attn_specs += [spec, spec]
        attn_args += [o, lse]
    return pl.pallas_call(
        _mix_kernel,
        grid=(S // tm,),
        in_specs=[row(D_MODEL), row(HGRN_WIDTH), row(HGRN_WIDTH), row(HGRN_WIDTH, 4),
                  row(FOURIER_WIDTH)] + attn_specs + [_const_spec((D_MODEL, D_MODEL))],
        out_specs=row(D_MODEL),
        out_shape=jax.ShapeDtypeStruct((S, D_MODEL), F32),
        scratch_shapes=[pltpu.VMEM((tm, LANES), F32)],
        compiler_params=_cparams(("parallel",)),
        name="mix_outproj",
    )(x, o_f, o_b, hg, y, *attn_args, w_out)


def _mlp_kernel(x_ref, nw_ref, wu_ref, wd_ref, fw_ref, out_ref, *, final):
    acc = x_ref[...]
    h = _rms(acc, nw_ref[...]).astype(BF16)
    for c in range(0, D_FF, FF_CHUNK):
        u = jnp.dot(h, wu_ref[:, c:c + FF_CHUNK], preferred_element_type=F32)
        u = jnp.square(jnp.maximum(u, 0.0)).astype(BF16)
        acc += jnp.dot(u, wd_ref[c:c + FF_CHUNK], preferred_element_type=F32)
    out_ref[...] = _rms(acc, fw_ref[...]) if final else acc


def _mlp(x, norm_w, w_up, w_down, final_w, final):
    S = x.shape[0]
    tm = ROW_TILE
    row = pl.BlockSpec((tm, D_MODEL), lambda i: (i, 0))
    return pl.pallas_call(
        functools.partial(_mlp_kernel, final=final),
        grid=(S // tm,),
        in_specs=[row, _const_spec((1, D_MODEL)), _const_spec((D_MODEL, D_FF)),
                  _const_spec((D_FF, D_MODEL)), _const_spec((1, D_MODEL))],
        out_specs=row,
        out_shape=jax.ShapeDtypeStruct((S, D_MODEL), F32),
        compiler_params=_cparams(("parallel",)),
        name="mlp",
    )(x, norm_w, w_up, w_down, final_w)


def _rope_tables(positions):
    inv_freq = ROPE_THETA ** (-jnp.arange(0, HEAD_DIM, 2, dtype=F32) / HEAD_DIM)
    ang = positions.astype(F32)[:, None] * inv_freq
    cos, sin = jnp.cos(ang), jnp.sin(ang)
    reps = LANES // HEAD_DIM
    cos_t = jnp.tile(cos, (1, 2 * reps))
    sin_t = jnp.tile(jnp.concatenate([-sin, sin], axis=1), (1, reps))
    return cos_t, sin_t


def kernel(x, positions, attn_norm_w, w_in, hgrn_lower_bounds, w_out, mlp_norm_w, w_up, w_down,
           final_norm_w):
    B, S, D = x.shape
    depth = w_in.shape[0]
    assert B == 1 and D == D_MODEL and S % (ATTN_TILE * max(DILATIONS)) == 0
    assert S % (HGRN_BLOCK * 2) == 0

    p_lb = jax.nn.softmax(hgrn_lower_bounds.astype(F32), axis=0)
    lb_all = jnp.cumsum(p_lb, axis=0) - p_lb[0:1]
    cos_t, sin_t = _rope_tables(positions[0])
    fconsts = _fourier_constants(S)
    w_in16, w_out16 = w_in.astype(BF16), w_out.astype(BF16)
    w_up16, w_down16 = w_up.astype(BF16), w_down.astype(BF16)
    final_w = final_norm_w.reshape(1, D)

    xs = x[0]
    for layer in range(depth):
        hg, fu, qkv = _inproj(xs, attn_norm_w[layer].reshape(1, D), w_in16[layer], cos_t, sin_t)
        o_f, o_b = _hgrn(hg, lb_all[layer])
        y = _fourier(fu, fconsts)
        attn = [_attn_pattern(q, k, v) for q, k, v in qkv]
        xs = _mix(xs, o_f, o_b, hg, y, attn, w_out16[layer])
        xs = _mlp(xs, mlp_norm_w[layer].reshape(1, D), w_up16[layer], w_down16[layer], final_w,
                  final=(layer == depth - 1))
    return xs[None]
```

```python
import functools
import math

import numpy as np
import jax
import jax.numpy as jnp
from jax import lax
from jax.experimental import pallas as pl
from jax.experimental.pallas import tpu as pltpu

F32 = jnp.float32
BF16 = jnp.bfloat16

D_MODEL = 1024
HEAD_DIM = 64
HGRN_WIDTH = 256
FOURIER_WIDTH = 256
FOURIER_GROUPS = 4
ATTN_WIDTH = 512
ATTN_HEADS = 8
HEAD_PAIRS = ATTN_HEADS // 2
LANES = 128
IN_WIDTH = 5 * HGRN_WIDTH + FOURIER_WIDTH + 3 * ATTN_WIDTH
D_FF = 4 * D_MODEL
DILATIONS = (1, 4, 16)
WINDOW_RADIUS = 64
ROPE_THETA = 10000.0
RMS_EPS = 1e-6
NEG_BIG = -1e30

HGRN_CHUNK = 64
HGRN_BLOCK = 256
ATTN_TILE = 1024
ATTN_QBLK = 128
ROW_TILE = 512
FF_CHUNK = 1024
FOURIER_PER_STEP = 8
VMEM_LIMIT = 56 * 1024 * 1024


def _cparams(sem):
    return pltpu.CompilerParams(dimension_semantics=sem, vmem_limit_bytes=VMEM_LIMIT)


def _const_spec(shape):
    nd = len(shape)
    return pl.BlockSpec(shape, lambda *_: (0,) * nd)


def _layer_spec(shape, layer):
    nd = len(shape)
    return pl.BlockSpec((None,) + tuple(shape), lambda *_: (layer,) + (0,) * nd)


def _sigmoid(x):
    return 1.0 / (1.0 + jnp.exp(-x))


def _inproj_kernel(x_ref, nw_ref, w_ref, cos_ref, sin_ref, hg_ref, fu_ref, *rest):
    qkv_refs, stage_ref = rest[:-1], rest[-1]
    tm = x_ref.shape[0]
    x = x_ref[...]
    ms = jnp.mean(x * x, axis=-1, keepdims=True)
    h = (x * lax.rsqrt(ms + RMS_EPS) * nw_ref[...]).astype(BF16)

    def proj(c0, c1):
        return jnp.dot(h, w_ref[:, c0:c1], preferred_element_type=F32)

    hg_w = 5 * HGRN_WIDTH
    for c in range(0, hg_w, HGRN_WIDTH):
        hg_ref[:, c:c + HGRN_WIDTH] = proj(c, c + HGRN_WIDTH)
    fu_ref[...] = proj(hg_w, hg_w + FOURIER_WIDTH)

    cos = cos_ref[...]
    sin = sin_ref[...]
    lane = lax.broadcasted_iota(jnp.int32, cos.shape, 1)
    low_half = (lane % HEAD_DIM) < (HEAD_DIM // 2)

    def rotary(t):
        partner = jnp.where(low_half, pltpu.roll(t, LANES - HEAD_DIM // 2, 1),
                            pltpu.roll(t, HEAD_DIM // 2, 1))
        return t * cos + partner * sin

    base = hg_w + FOURIER_WIDTH
    for which in range(3):
        for hp0 in range(0, HEAD_PAIRS, 2):
            c = base + which * ATTN_WIDTH + hp0 * LANES
            both = proj(c, c + 2 * LANES)
            for hp in (hp0, hp0 + 1):
                val = both[:, (hp - hp0) * LANES:(hp - hp0 + 1) * LANES]
                if which == 0:
                    val = rotary(val) * (HEAD_DIM ** -0.5)
                elif which == 1:
                    val = rotary(val)
                stage_ref[...] = val
                for di, d in enumerate(DILATIONS):
                    out_ref = qkv_refs[di * 3 + which]
                    for r in range(d):
                        out_ref[hp, r] = stage_ref[pl.ds(r, tm // d, stride=d), :].astype(BF16)


def _inproj(x, norm_w, w_in, layer, cos_t, sin_t):
    S = x.shape[0]
    tm = ROW_TILE
    row = lambda w: pl.BlockSpec((tm, w), lambda i: (i, 0))
    qkv_specs, qkv_shapes = [], []
    for d in DILATIONS:
        for _ in range(3):
            qkv_specs.append(pl.BlockSpec((HEAD_PAIRS, d, tm // d, LANES), lambda i: (0, 0, i, 0)))
            qkv_shapes.append(jax.ShapeDtypeStruct((HEAD_PAIRS, d, S // d, LANES), BF16))
    outs = pl.pallas_call(
        _inproj_kernel,
        grid=(S // tm,),
        in_specs=[row(D_MODEL), _const_spec((1, D_MODEL)),
                  _layer_spec((D_MODEL, IN_WIDTH), layer),
                  row(LANES), row(LANES)],
        out_specs=[row(5 * HGRN_WIDTH), row(FOURIER_WIDTH)] + qkv_specs,
        out_shape=[jax.ShapeDtypeStruct((S, 5 * HGRN_WIDTH), F32),
                   jax.ShapeDtypeStruct((S, FOURIER_WIDTH), F32)] + qkv_shapes,
        scratch_shapes=[pltpu.VMEM((tm, LANES), F32)],
        compiler_params=_cparams(("parallel",)),
        name="inproj",
    )(x, norm_w, w_in, cos_t, sin_t)
    return outs[0], outs[1], [outs[2 + 3 * i:5 + 3 * i] for i in range(len(DILATIONS))]


def _split3(x):
    hi = x.astype(BF16)
    r = x - hi.astype(F32)
    mid = r.astype(BF16)
    lo = (r - mid.astype(F32)).astype(BF16)
    return hi, mid, lo


def _hgrn_chunk(q_in, v, z, lb, state_t, reverse):
    C = q_in.shape[0]
    W = q_in.shape[1]
    f = lb + (1.0 - lb) * _sigmoid(z)
    log_f = jnp.log(f)
    kk = 1.0 - f
    q = q_in * _sigmoid(q_in)

    r_i = lax.broadcasted_iota(jnp.int32, (C, C), 0)
    c_i = lax.broadcasted_iota(jnp.int32, (C, C), 1)
    causal = (c_i >= r_i) if reverse else (c_i <= r_i)
    ones_tri = causal.astype(BF16)
    hi, mid, lo = _split3(log_f)
    b = (jnp.dot(ones_tri, hi, preferred_element_type=F32)
         + jnp.dot(ones_tri, mid, preferred_element_type=F32)
         + jnp.dot(ones_tri, lo, preferred_element_type=F32))
    edge = 0 if reverse else C - 1
    b_edge = b[edge:edge + 1]
    b_mid = b[C // 2:C // 2 + 1]

    q_inter = (q * jnp.exp(b)).astype(BF16)
    q_loc = (q * jnp.exp(b - b_mid)).astype(BF16)
    k_loc = (kk * jnp.exp(b_mid - b)).astype(BF16)
    k_edge = (kk * jnp.exp(b_edge - b)).astype(BF16)
    v16 = v.astype(BF16)

    contract_lanes = (((1,), (1,)), ((), ()))
    inter = lax.dot_general(q_inter, state_t.astype(BF16), contract_lanes,
                            preferred_element_type=F32)

    lane = lax.broadcasted_iota(jnp.int32, (C, W), 1)
    zero16 = jnp.zeros((C, W), BF16)
    scores = []
    v_rows = []
    for h in range(W // HEAD_DIM):
        in_head = (lane // HEAD_DIM) == h
        a = lax.dot_general(jnp.where(in_head, q_loc, zero16), k_loc, contract_lanes,
                            preferred_element_type=F32)
        scores.append(jnp.where(causal, a, 0.0).astype(BF16))
        v_rows.append(jnp.where(in_head, v16, zero16))
    intra = jnp.dot(jnp.concatenate(scores, axis=1), jnp.concatenate(v_rows, axis=0),
                    preferred_element_type=F32)

    upd = lax.dot_general(v16, k_edge, (((0,), (0,)), ((), ())),
                          preferred_element_type=F32)
    s_r = lax.broadcasted_iota(jnp.int32, (W, W), 0) // HEAD_DIM
    s_c = lax.broadcasted_iota(jnp.int32, (W, W), 1) // HEAD_DIM
    new_state = state_t * jnp.exp(b_edge) + jnp.where(s_r == s_c, upd, 0.0)
    return inter + intra, new_state


def _hgrn_kernel(qf_ref, vf_ref, zf_ref, qb_ref, vb_ref, zb_ref, lb_ref, of_ref, ob_ref,
                 sf_ref, sb_ref):
    @pl.when(pl.program_id(0) == 0)
    def _():
        sf_ref[...] = jnp.zeros_like(sf_ref)
        sb_ref[...] = jnp.zeros_like(sb_ref)

    C = HGRN_CHUNK
    n = HGRN_BLOCK // C
    lb_f = lb_ref[0:1]
    lb_b = lb_ref[1:2]
    sf = sf_ref[...]
    sb = sb_ref[...]
    for j in range(n):
        rows = slice(j * C, (j + 1) * C)
        o, sf = _hgrn_chunk(qf_ref[rows], vf_ref[rows], zf_ref[rows], lb_f, sf, False)
        of_ref[rows] = o
        rows = slice((n - 1 - j) * C, (n - j) * C)
        o, sb = _hgrn_chunk(qb_ref[rows], vb_ref[rows], zb_ref[rows], lb_b, sb, True)
        ob_ref[rows] = o
    sf_ref[...] = sf
    sb_ref[...] = sb


def _hgrn(hg, lb):
    S = hg.shape[0]
    tb = HGRN_BLOCK
    n = S // tb
    W = HGRN_WIDTH
    fwd = lambda col: pl.BlockSpec((tb, W), lambda i: (i, col))
    bwd = lambda col: pl.BlockSpec((tb, W), lambda i: (n - 1 - i, col))
    return pl.pallas_call(
        _hgrn_kernel,
        grid=(n,),
        in_specs=[fwd(0), fwd(1), fwd(2), bwd(0), bwd(1), bwd(3), _const_spec((2, W))],
        out_specs=[fwd(0), bwd(0)],
        out_shape=[jax.ShapeDtypeStruct((S, W), F32)] * 2,
        scratch_shapes=[pltpu.VMEM((W, W), F32)] * 2,
        compiler_params=_cparams(("arbitrary",)),
        name="hgrn_scan",
    )(hg, hg, hg, hg, hg, hg, lb)


def _fourier_constants(S):
    n1 = 1 << (int(math.log2(S)) // 2)
    n2 = S // n1
    cg = FOURIER_WIDTH // FOURIER_GROUPS
    j = np.arange(cg)
    ang = 2.0 * np.pi * np.outer(j, j) / cg
    eye = np.eye(FOURIER_GROUPS)
    ch = np.concatenate([np.kron(eye, np.cos(ang)), -np.kron(eye, np.sin(ang))], axis=1)
    a1 = 2.0 * np.pi * np.outer(np.arange(n1), np.arange(n1)) / n1
    fr, fi = np.cos(a1), -np.sin(a1)
    fa = np.block([[fr, -fi], [fi, fr]])
    a2 = 2.0 * np.pi * np.outer(np.arange(n2), np.arange(n2)) / n2
    fb = np.concatenate([np.cos(a2), np.sin(a2)], axis=1)
    at = 2.0 * np.pi * np.outer(np.arange(n2), np.arange(n1)) / S
    twr = np.repeat(np.cos(at)[:, :, None], LANES, axis=2)
    twi = np.repeat(-np.sin(at)[:, :, None], LANES, axis=2)
    return (n1, n2, jnp.asarray(ch, F32), jnp.asarray(fa, F32), jnp.asarray(fb, F32),
            jnp.asarray(twr, F32), jnp.asarray(twi, F32))


def _fourier_a_kernel(u_ref, ch_ref, fa_ref, twr_ref, twi_ref, z_ref):
    W = FOURIER_WIDTH
    ch = ch_ref[...].astype(BF16)
    fa = fa_ref[...].astype(BF16)
    for s in range(u_ref.shape[1]):
        x = u_ref[:, s, :].astype(BF16)
        uc = jnp.dot(x, ch, preferred_element_type=F32)
        stacked = jnp.concatenate([uc[:, :W], uc[:, W:]], axis=0).astype(BF16)
        a = jnp.dot(fa, stacked, preferred_element_type=F32)
        n1 = a.shape[0] // 2
        ar, ai = a[:n1], a[n1:]
        twr = jnp.concatenate([twr_ref[s]] * (W // LANES), axis=1)
        twi = jnp.concatenate([twi_ref[s]] * (W // LANES), axis=1)
        z_ref[0, :, s, :] = ar * twr - ai * twi
        z_ref[1, :, s, :] = ar * twi + ai * twr


def _fourier_b_kernel(z_ref, fb_ref, y_ref, *, scale):
    fb = fb_ref[...].astype(BF16)
    for p in range(z_ref.shape[1]):
        rhs = jnp.concatenate([z_ref[0, p], z_ref[1, p]], axis=0).astype(BF16)
        y_ref[:, p, :] = jnp.dot(fb, rhs, preferred_element_type=F32) * scale


def _fourier(u, consts):
    S, W = u.shape
    n1, n2, ch, fa, fb, twr, twi = consts
    per = FOURIER_PER_STEP
    z = pl.pallas_call(
        _fourier_a_kernel,
        grid=(n2 // per,),
        in_specs=[pl.BlockSpec((n1, per, W), lambda i: (0, i, 0)), _const_spec(ch.shape),
                  _const_spec(fa.shape), pl.BlockSpec((per, n1, LANES), lambda i: (i, 0, 0)),
                  pl.BlockSpec((per, n1, LANES), lambda i: (i, 0, 0))],
        out_specs=pl.BlockSpec((2, n1, per, W), lambda i: (0, 0, i, 0)),
        out_shape=jax.ShapeDtypeStruct((2, n1, n2, W), F32),
        compiler_params=_cparams(("parallel",)),
        name="fourier_a",
    )(u.reshape(n1, n2, W), ch, fa, twr, twi)
    cg = W // FOURIER_GROUPS
    y = pl.pallas_call(
        functools.partial(_fourier_b_kernel, scale=1.0 / math.sqrt(S * cg)),
        grid=(n1 // per,),
        in_specs=[pl.BlockSpec((2, per, n2, W), lambda i: (0, i, 0, 0)), _const_spec(fb.shape)],
        out_specs=pl.BlockSpec((n2, per, W), lambda i: (0, i, 0)),
        out_shape=jax.ShapeDtypeStruct((n2, n1, W), F32),
        compiler_params=_cparams(("parallel",)),
        name="fourier_b",
    )(z, fb)
    return y.reshape(S, W)


def _attn_kernel(q_ref, k_ref, kp_ref, kn_ref, v_ref, vp_ref, vn_ref, o_ref, lse_ref,
                 kext_ref, vaug_ref):
    R = WINDOW_RADIUS
    rows = q_ref.shape[0]
    QB = ATTN_QBLK
    KW = QB + 2 * R
    t = pl.program_id(1)
    nt = pl.num_programs(1)

    lane = lax.broadcasted_iota(jnp.int32, (QB, LANES), 1)
    head0 = lane < HEAD_DIM

    kext_ref[0:R] = kp_ref[...]
    kext_ref[R:R + rows] = k_ref[...]
    kext_ref[R + rows:] = kn_ref[...]
    for h in range(2):
        for lo, hi, src in ((0, R, vp_ref), (R, R + rows, v_ref), (R + rows, rows + 2 * R, vn_ref)):
            val = src[...]
            in_head0 = lax.broadcasted_iota(jnp.int32, val.shape, 1) < HEAD_DIM
            mine = in_head0 if h == 0 else ~in_head0
            vaug_ref[h, lo:hi, 0:LANES] = jnp.where(mine, val, jnp.zeros_like(val))
            vaug_ref[h, lo:hi, LANES:] = mine.astype(BF16)

    r_i = lax.broadcasted_iota(jnp.int32, (QB, KW), 0)
    c_i = lax.broadcasted_iota(jnp.int32, (QB, KW), 1)
    band = jnp.where((c_i >= r_i) & (c_i <= r_i + 2 * R), 0.0, NEG_BIG)
    first_bias = band + jnp.where((t == 0) & (c_i < R), NEG_BIG, 0.0)
    last_bias = band + jnp.where((t == nt - 1) & (c_i >= QB + R), NEG_BIG, 0.0)
    zero16 = jnp.zeros((QB, LANES), BF16)
    contract_lanes = (((1,), (1,)), ((), ()))

    nblk = rows // QB
    for jb in range(nblk):
        q = q_ref[jb * QB:(jb + 1) * QB]
        kwin = kext_ref[jb * QB:jb * QB + KW]
        bias = first_bias if jb == 0 else (last_bias if jb == nblk - 1 else band)
        q2 = jnp.concatenate([jnp.where(head0, q, zero16), jnp.where(head0, zero16, q)], axis=0)
        s2 = lax.dot_general(q2, kwin, contract_lanes, preferred_element_type=F32)
        tot = None
        maxes = []
        for h in range(2):
            s = s2[h * QB:(h + 1) * QB] + bias
            m = jnp.max(s, axis=-1, keepdims=True)
            p = jnp.exp(s - m).astype(BF16)
            part = jnp.dot(p, vaug_ref[h, jb * QB:jb * QB + KW], preferred_element_type=F32)
            tot = part if tot is None else tot + part
            maxes.append(m)
        l = tot[:, LANES:]
        o_ref[jb * QB:(jb + 1) * QB] = (tot[:, :LANES] / l).astype(BF16)
        lse_ref[jb * QB:(jb + 1) * QB] = jnp.where(head0, maxes[0], maxes[1]) + jnp.log(l)


def _attn_pattern(q, k, v):
    P, d, L, _ = q.shape
    rows = ATTN_TILE
    assert L % rows == 0 and rows % ATTN_QBLK == 0 and rows // ATTN_QBLK >= 2
    R = WINDOW_RADIUS
    halo_per_tile = rows // R
    n_halo = L // R
    flat = lambda a: a.reshape(P * d, L, LANES)
    main = pl.BlockSpec((None, rows, LANES), lambda b, t: (b, t, 0))
    prev = pl.BlockSpec((None, R, LANES),
                        lambda b, t: (b, jnp.maximum(t * halo_per_tile - 1, 0), 0))
    nxt = pl.BlockSpec((None, R, LANES),
                       lambda b, t: (b, jnp.minimum((t + 1) * halo_per_tile, n_halo - 1), 0))
    o, lse = pl.pallas_call(
        _attn_kernel,
        grid=(P * d, L // rows),
        in_specs=[main, main, prev, nxt, main, prev, nxt],
        out_specs=[main, main],
        out_shape=[jax.ShapeDtypeStruct((P * d, L, LANES), BF16),
                   jax.ShapeDtypeStruct((P * d, L, LANES), F32)],
        scratch_shapes=[pltpu.VMEM((rows + 2 * R, LANES), BF16),
                        pltpu.VMEM((2, rows + 2 * R, 2 * LANES), BF16)],
        compiler_params=_cparams(("parallel", "parallel")),
        name=f"dilated_attn_d{d}",
    )(flat(q), flat(k), flat(k), flat(k), flat(v), flat(v), flat(v))
    return o.reshape(P, d, L, LANES), lse.reshape(P, d, L, LANES)


def _rms(x, w):
    ms = jnp.mean(x * x, axis=-1, keepdims=True)
    return x * lax.rsqrt(ms + RMS_EPS) * w


def _mix_kernel(x_ref, of_ref, ob_ref, g_ref, y_ref, *rest):
    attn_refs, wo_ref, out_ref, stage_ref = rest[:-3], rest[-3], rest[-2], rest[-1]
    tm = x_ref.shape[0]
    o = of_ref[...] + ob_ref[...]
    W = o.shape[1]
    r_h = lax.broadcasted_iota(jnp.int32, (W, W), 0) // HEAD_DIM
    c_h = lax.broadcasted_iota(jnp.int32, (W, W), 1) // HEAD_DIM
    head_sum = (r_h == c_h).astype(BF16)
    sq_hi, sq_mid, _ = _split3(o * o)
    ms = (jnp.dot(sq_hi, head_sum, preferred_element_type=F32)
          + jnp.dot(sq_mid, head_sum, preferred_element_type=F32)) * (1.0 / HEAD_DIM)
    g = g_ref[...]
    o_a = o * lax.rsqrt(ms + RMS_EPS) * (g * _sigmoid(g))

    acc = x_ref[...]
    acc += jnp.dot(o_a.astype(BF16), wo_ref[0:W], preferred_element_type=F32)
    acc += jnp.dot(y_ref[...].astype(BF16), wo_ref[W:2 * W], preferred_element_type=F32)

    def natural(ref, hp, d):
        if d == 1:
            return ref[hp, 0].astype(F32)
        for r in range(d):
            stage_ref[pl.ds(r, tm // d, stride=d), :] = ref[hp, r].astype(F32)
        return stage_ref[...]

    for hp in range(HEAD_PAIRS):
        outs = [natural(attn_refs[2 * i], hp, d) for i, d in enumerate(DILATIONS)]
        lses = [natural(attn_refs[2 * i + 1], hp, d) for i, d in enumerate(DILATIONS)]
        m = functools.reduce(jnp.maximum, lses)
        ws = [jnp.exp(l - m) for l in lses]
        o_c = sum(w * ov for w, ov in zip(ws, outs)) / sum(ws)
        c = 2 * W + hp * LANES
        acc += jnp.dot(o_c.astype(BF16), wo_ref[c:c + LANES], preferred_element_type=F32)
    out_ref[...] = acc


def _mix(x, o_f, o_b, hg, y, attn, w_out, layer):
    S = x.shape[0]
    tm = ROW_TILE
    row = lambda w, col=0: pl.BlockSpec((tm, w), lambda i: (i, col))
    attn_specs, attn_args = [], []
    for d, (o, lse) in zip(DILATIONS, attn):
        spec = pl.BlockSpec((HEAD_PAIRS, d, tm // d, LANES), lambda i: (0, 0, i, 0))
        attn_specs += [spec, spec]
        attn_args += [o, lse]
    return pl.pallas_call(
        _mix_kernel,
        grid=(S // tm,),
        in_specs=[row(D_MODEL), row(HGRN_WIDTH), row(HGRN_WIDTH), row(HGRN_WIDTH, 4),
                  row(FOURIER_WIDTH)] + attn_specs
                 + [_layer_spec((D_MODEL, D_MODEL), layer)],
        out_specs=row(D_MODEL),
        out_shape=jax.ShapeDtypeStruct((S, D_MODEL), F32),
        scratch_shapes=[pltpu.VMEM((tm, LANES), F32)],
        compiler_params=_cparams(("parallel",)),
        name="mix_outproj",
    )(x, o_f, o_b, hg, y, *attn_args, w_out)


def _mlp_kernel(x_ref, nw_ref, wu_ref, wd_ref, fw_ref, out_ref, *, final):
    acc = x_ref[...]
    h = _rms(acc, nw_ref[...]).astype(BF16)
    for c in range(0, D_FF, FF_CHUNK):
        u = jnp.dot(h, wu_ref[:, c:c + FF_CHUNK], preferred_element_type=F32)
        u = jnp.square(jnp.maximum(u, 0.0)).astype(BF16)
        acc += jnp.dot(u, wd_ref[c:c + FF_CHUNK], preferred_element_type=F32)
    out_ref[...] = _rms(acc, fw_ref[...]) if final else acc


def _mlp(x, norm_w, w_up, w_down, layer, final_w, final):
    S = x.shape[0]
    tm = ROW_TILE
    row = pl.BlockSpec((tm, D_MODEL), lambda i: (i, 0))
    return pl.pallas_call(
        functools.partial(_mlp_kernel, final=final),
        grid=(S // tm,),
        in_specs=[row, _const_spec((1, D_MODEL)), _layer_spec((D_MODEL, D_FF), layer),
                  _layer_spec((D_FF, D_MODEL), layer), _const_spec((1, D_MODEL))],
        out_specs=row,
        out_shape=jax.ShapeDtypeStruct((S, D_MODEL), F32),
        compiler_params=_cparams(("parallel",)),
        name="mlp",
    )(x, norm_w, w_up, w_down, final_w)


def _rope_tables(positions):
    inv_freq = ROPE_THETA ** (-jnp.arange(0, HEAD_DIM, 2, dtype=F32) / HEAD_DIM)
    ang = positions.astype(F32)[:, None] * inv_freq
    cos, sin = jnp.cos(ang), jnp.sin(ang)
    reps = LANES // HEAD_DIM
    cos_t = jnp.tile(cos, (1, 2 * reps))
    sin_t = jnp.tile(jnp.concatenate([-sin, sin], axis=1), (1, reps))
    return cos_t, sin_t


def kernel(x, positions, attn_norm_w, w_in, hgrn_lower_bounds, w_out, mlp_norm_w, w_up, w_down,
           final_norm_w):
    B, S, D = x.shape
    depth = w_in.shape[0]
    assert B == 1 and D == D_MODEL and S % (ATTN_TILE * max(DILATIONS)) == 0
    assert S % (HGRN_BLOCK * 2) == 0

    p_lb = jax.nn.softmax(hgrn_lower_bounds.astype(F32), axis=0)
    lb_all = jnp.cumsum(p_lb, axis=0) - p_lb[0:1]
    cos_t, sin_t = _rope_tables(positions[0])
    fconsts = _fourier_constants(S)
    w_in16, w_out16 = w_in.astype(BF16), w_out.astype(BF16)
    w_up16, w_down16 = w_up.astype(BF16), w_down.astype(BF16)
    final_w = final_norm_w.reshape(1, D)

    xs = x[0]
    for layer in range(depth):
        hg, fu, qkv = _inproj(xs, attn_norm_w[layer].reshape(1, D), w_in16, layer, cos_t, sin_t)
        o_f, o_b = _hgrn(hg, lb_all[layer])
        y = _fourier(fu, fconsts)
        attn = [_attn_pattern(q, k, v) for q, k, v in qkv]
        xs = _mix(xs, o_f, o_b, hg, y, attn, w_out16, layer)
        xs = _mlp(xs, mlp_norm_w[layer].reshape(1, D), w_up16, w_down16, layer, final_w,
                  final=(layer == depth - 1))
    return xs[None]
```

```python
import functools
import math

import numpy as np
import jax
import jax.numpy as jnp
from jax import lax
from jax.experimental import pallas as pl
from jax.experimental.pallas import tpu as pltpu

F32 = jnp.float32
BF16 = jnp.bfloat16

D_MODEL = 1024
HEAD_DIM = 64
HGRN_WIDTH = 256
FOURIER_WIDTH = 256
FOURIER_GROUPS = 4
ATTN_WIDTH = 512
ATTN_HEADS = 8
HEAD_PAIRS = ATTN_HEADS // 2
LANES = 128
IN_WIDTH = 5 * HGRN_WIDTH + FOURIER_WIDTH + 3 * ATTN_WIDTH
D_FF = 4 * D_MODEL
DILATIONS = (1, 4, 16)
WINDOW_RADIUS = 64
ROPE_THETA = 10000.0
RMS_EPS = 1e-6
NEG_BIG = -1e30

HGRN_CHUNK = 64
HGRN_BLOCK = 256
ATTN_TILE = 2048
ATTN_QBLK = 128
ROW_TILE = 512
FF_CHUNK = 1024
FOURIER_PER_STEP = 8
VMEM_LIMIT = 56 * 1024 * 1024


def _cparams(sem):
    return pltpu.CompilerParams(dimension_semantics=sem, vmem_limit_bytes=VMEM_LIMIT)


def _const_spec(shape):
    nd = len(shape)
    return pl.BlockSpec(shape, lambda *_: (0,) * nd)


def _layer_spec(shape, layer):
    nd = len(shape)
    return pl.BlockSpec((None,) + tuple(shape), lambda *_: (layer,) + (0,) * nd)


def _sigmoid(x):
    return 1.0 / (1.0 + jnp.exp(-x))


def _inproj_kernel(x_ref, nw_ref, w_ref, cos_ref, sin_ref, hg_ref, fu_ref, *rest):
    n_pat = len(DILATIONS)
    qkv_refs, stage_refs = rest[:3 * n_pat], rest[3 * n_pat:]
    tm = x_ref.shape[0]

    def store_regrouped(val, which, hp):
        stage_refs[0][0] = val
        qkv_refs[which][hp, 0] = val.astype(BF16)
        for i in range(1, n_pat):
            dp, d = DILATIONS[i - 1], DILATIONS[i]
            ratio = d // dp
            for rp in range(dp):
                for r2 in range(ratio):
                    rows = stage_refs[i - 1][rp, pl.ds(r2, tm // d, stride=ratio), :]
                    r = r2 * dp + rp
                    if i + 1 < n_pat:
                        stage_refs[i][r] = rows
                    qkv_refs[3 * i + which][hp, r] = rows.astype(BF16)

    x = x_ref[...]
    ms = jnp.mean(x * x, axis=-1, keepdims=True)
    h = (x * lax.rsqrt(ms + RMS_EPS) * nw_ref[...]).astype(BF16)

    def proj(c0, c1):
        return jnp.dot(h, w_ref[:, c0:c1], preferred_element_type=F32)

    hg_w = 5 * HGRN_WIDTH
    cos = cos_ref[...]
    sin = sin_ref[...]
    lane = lax.broadcasted_iota(jnp.int32, cos.shape, 1)
    low_half = (lane % HEAD_DIM) < (HEAD_DIM // 2)

    def rotary(t):
        partner = jnp.where(low_half, pltpu.roll(t, LANES - HEAD_DIM // 2, 1),
                            pltpu.roll(t, HEAD_DIM // 2, 1))
        return t * cos + partner * sin

    base = hg_w + FOURIER_WIDTH
    for which in range(3):
        for hp0 in range(0, HEAD_PAIRS, 2):
            c = base + which * ATTN_WIDTH + hp0 * LANES
            both = proj(c, c + 2 * LANES)
            for hp in (hp0, hp0 + 1):
                val = both[:, (hp - hp0) * LANES:(hp - hp0 + 1) * LANES]
                if which == 0:
                    val = rotary(val) * (HEAD_DIM ** -0.5)
                elif which == 1:
                    val = rotary(val)
                store_regrouped(val, which, hp)

    for c in range(0, hg_w, HGRN_WIDTH):
        hg_ref[:, c:c + HGRN_WIDTH] = proj(c, c + HGRN_WIDTH)
    fu_ref[...] = proj(hg_w, hg_w + FOURIER_WIDTH)


def _inproj(x, norm_w, w_in, layer, cos_t, sin_t):
    S = x.shape[0]
    tm = ROW_TILE
    row = lambda w: pl.BlockSpec((tm, w), lambda i: (i, 0))
    qkv_specs, qkv_shapes = [], []
    for d in DILATIONS:
        for _ in range(3):
            qkv_specs.append(pl.BlockSpec((HEAD_PAIRS, d, tm // d, LANES), lambda i: (0, 0, i, 0)))
            qkv_shapes.append(jax.ShapeDtypeStruct((HEAD_PAIRS, d, S // d, LANES), BF16))
    outs = pl.pallas_call(
        _inproj_kernel,
        grid=(S // tm,),
        in_specs=[row(D_MODEL), _const_spec((1, D_MODEL)),
                  _layer_spec((D_MODEL, IN_WIDTH), layer),
                  row(LANES), row(LANES)],
        out_specs=[row(5 * HGRN_WIDTH), row(FOURIER_WIDTH)] + qkv_specs,
        out_shape=[jax.ShapeDtypeStruct((S, 5 * HGRN_WIDTH), F32),
                   jax.ShapeDtypeStruct((S, FOURIER_WIDTH), F32)] + qkv_shapes,
        scratch_shapes=[pltpu.VMEM((d, tm // d, LANES), F32) for d in DILATIONS[:-1]],
        compiler_params=_cparams(("parallel",)),
        name="inproj",
    )(x, norm_w, w_in, cos_t, sin_t)
    return outs[0], outs[1], [outs[2 + 3 * i:5 + 3 * i] for i in range(len(DILATIONS))]


def _split3(x):
    hi = x.astype(BF16)
    r = x - hi.astype(F32)
    mid = r.astype(BF16)
    lo = (r - mid.astype(F32)).astype(BF16)
    return hi, mid, lo


def _tanh_sigmoid(x):
    return 0.5 * jnp.tanh(0.5 * x) + 0.5


HGRN_SLAB = 2 * HEAD_DIM


def _hgrn_gates(q_in, z, lb, reverse):
    C = q_in.shape[0]
    f = lb + (1.0 - lb) * _tanh_sigmoid(z)
    kk = 1.0 - f
    q = q_in * _tanh_sigmoid(q_in)
    r_i = lax.broadcasted_iota(jnp.int32, (C, C), 0)
    c_i = lax.broadcasted_iota(jnp.int32, (C, C), 1)
    causal = (c_i >= r_i) if reverse else (c_i <= r_i)
    ones_tri = causal.astype(BF16)
    hi, mid, lo = _split3(jnp.log(f))
    b = (jnp.dot(ones_tri, hi, preferred_element_type=F32)
         + jnp.dot(ones_tri, mid, preferred_element_type=F32)
         + jnp.dot(ones_tri, lo, preferred_element_type=F32))
    return q, kk, b, causal


def _hgrn_decays(q, kk, b, reverse):
    C = q.shape[0]
    edge = 0 if reverse else C - 1
    b_edge = b[edge:edge + 1]
    b_mid = b[C // 2:C // 2 + 1]
    q_loc = q * jnp.exp(b - b_mid)
    k_loc = kk * jnp.exp(b_mid - b)
    q_inter = q_loc * jnp.exp(b_mid)
    k_edge = k_loc * jnp.exp(b_edge - b_mid)
    return (q_inter.astype(BF16), q_loc.astype(BF16), k_loc.astype(BF16), k_edge.astype(BF16),
            jnp.exp(b_edge))


def _hgrn_local(q_loc, k_loc, k_edge, v16, causal):
    C, W = q_loc.shape
    contract_lanes = (((1,), (1,)), ((), ()))
    lane = lax.broadcasted_iota(jnp.int32, (C, W), 1)
    zero16 = jnp.zeros((C, W), BF16)
    scores = []
    v_rows = []
    for h in range(W // HEAD_DIM):
        in_head = (lane // HEAD_DIM) == h
        a = lax.dot_general(jnp.where(in_head, q_loc, zero16), k_loc, contract_lanes,
                            preferred_element_type=F32)
        scores.append(jnp.where(causal, a, 0.0).astype(BF16))
        v_rows.append(jnp.where(in_head, v16, zero16))
    intra = jnp.dot(jnp.concatenate(scores, axis=1), jnp.concatenate(v_rows, axis=0),
                    preferred_element_type=F32)
    s_r = lax.broadcasted_iota(jnp.int32, (HGRN_SLAB, HGRN_SLAB), 0) // HEAD_DIM
    s_c = lax.broadcasted_iota(jnp.int32, (HGRN_SLAB, HGRN_SLAB), 1) // HEAD_DIM
    upd = []
    for c in range(0, W, HGRN_SLAB):
        u = lax.dot_general(v16[:, c:c + HGRN_SLAB], k_edge[:, c:c + HGRN_SLAB],
                            (((0,), (0,)), ((), ())), preferred_element_type=F32)
        upd.append(jnp.where(s_r == s_c, u, 0.0))
    return intra, upd


def _hgrn_kernel(qf_ref, vf_ref, zf_ref, qb_ref, vb_ref, zb_ref, lb_ref, of_ref, ob_ref,
                 sf_ref, sb_ref):
    @pl.when(pl.program_id(0) == 0)
    def _():
        sf_ref[...] = jnp.zeros_like(sf_ref)
        sb_ref[...] = jnp.zeros_like(sb_ref)

    C = HGRN_CHUNK
    n = HGRN_BLOCK // C
    chains = [((qf_ref, vf_ref, zf_ref, of_ref, sf_ref), lb_ref[0:1], False, list(range(n))),
              ((qb_ref, vb_ref, zb_ref, ob_ref, sb_ref), lb_ref[1:2], True,
               list(range(n - 1, -1, -1)))]
    jobs = []
    for refs, lb, reverse, order in chains:
        for j in order:
            rows = slice(j * C, (j + 1) * C)
            jobs.append(dict(refs=refs, rows=rows, reverse=reverse,
                             gates=_hgrn_gates(refs[0][rows], refs[2][rows], lb, reverse)))
    for job in jobs:
        q, kk, b, causal = job["gates"]
        job["decays"] = _hgrn_decays(q, kk, b, job["reverse"])
        job["causal"] = causal
    for job in jobs:
        _, q_loc, k_loc, k_edge, _ = job["decays"]
        v16 = job["refs"][1][job["rows"]].astype(BF16)
        job["local"] = _hgrn_local(q_loc, k_loc, k_edge, v16, job["causal"])

    contract_lanes = (((1,), (1,)), ((), ()))
    W = HGRN_WIDTH
    for refs, _, _, _ in chains:
        state = [refs[4][i] for i in range(W // HGRN_SLAB)]
        for job in jobs:
            if job["refs"] is not refs:
                continue
            q_inter, decay = job["decays"][0], job["decays"][4]
            intra, upd = job["local"]
            inter = [lax.dot_general(q_inter[:, i * HGRN_SLAB:(i + 1) * HGRN_SLAB],
                                     state[i].astype(BF16), contract_lanes,
                                     preferred_element_type=F32) for i in range(len(state))]
            refs[3][job["rows"]] = jnp.concatenate(inter, axis=1) + intra
            state = [state[i] * decay[:, i * HGRN_SLAB:(i + 1) * HGRN_SLAB] + upd[i]
                     for i in range(len(state))]
        for i, s_i in enumerate(state):
            refs[4][i] = s_i


def _hgrn(hg, lb):
    S = hg.shape[0]
    tb = HGRN_BLOCK
    n = S // tb
    W = HGRN_WIDTH
    fwd = lambda col: pl.BlockSpec((tb, W), lambda i: (i, col))
    bwd = lambda col: pl.BlockSpec((tb, W), lambda i: (n - 1 - i, col))
    state = pltpu.VMEM((W // HGRN_SLAB, HGRN_SLAB, HGRN_SLAB), F32)
    return pl.pallas_call(
        _hgrn_kernel,
        grid=(n,),
        in_specs=[fwd(0), fwd(1), fwd(2), bwd(0), bwd(1), bwd(3), _const_spec((2, W))],
        out_specs=[fwd(0), bwd(0)],
        out_shape=[jax.ShapeDtypeStruct((S, W), F32)] * 2,
        scratch_shapes=[state, state],
        compiler_params=_cparams(("arbitrary",)),
        name="hgrn_scan",
    )(hg, hg, hg, hg, hg, hg, lb)


def _fourier_constants(S):
    n1 = 1 << (int(math.log2(S)) // 2)
    n2 = S // n1
    cg = FOURIER_WIDTH // FOURIER_GROUPS
    j = np.arange(cg)
    ang = 2.0 * np.pi * np.outer(j, j) / cg
    eye = np.eye(FOURIER_GROUPS)
    ch = np.concatenate([np.kron(eye, np.cos(ang)), -np.kron(eye, np.sin(ang))], axis=1)
    a1 = 2.0 * np.pi * np.outer(np.arange(n1), np.arange(n1)) / n1
    fr, fi = np.cos(a1), -np.sin(a1)
    fa = np.block([[fr, -fi], [fi, fr]])
    a2 = 2.0 * np.pi * np.outer(np.arange(n2), np.arange(n2)) / n2
    fb = np.concatenate([np.cos(a2), np.sin(a2)], axis=1)
    at = 2.0 * np.pi * np.outer(np.arange(n2), np.arange(n1)) / S
    twr = np.repeat(np.cos(at)[:, :, None], LANES, axis=2)
    twi = np.repeat(-np.sin(at)[:, :, None], LANES, axis=2)
    return (n1, n2, jnp.asarray(ch, F32), jnp.asarray(fa, F32), jnp.asarray(fb, F32),
            jnp.asarray(twr, F32), jnp.asarray(twi, F32))


def _fourier_a_kernel(u_ref, ch_ref, fa_ref, twr_ref, twi_ref, z_ref):
    W = FOURIER_WIDTH
    ch = ch_ref[...].astype(BF16)
    fa = fa_ref[...].astype(BF16)
    for s in range(u_ref.shape[1]):
        x = u_ref[:, s, :].astype(BF16)
        uc = jnp.dot(x, ch, preferred_element_type=F32)
        stacked = jnp.concatenate([uc[:, :W], uc[:, W:]], axis=0).astype(BF16)
        a = jnp.dot(fa, stacked, preferred_element_type=F32)
        n1 = a.shape[0] // 2
        ar, ai = a[:n1], a[n1:]
        twr = jnp.concatenate([twr_ref[s]] * (W // LANES), axis=1)
        twi = jnp.concatenate([twi_ref[s]] * (W // LANES), axis=1)
        z_ref[0, :, s, :] = ar * twr - ai * twi
        z_ref[1, :, s, :] = ar * twi + ai * twr


def _fourier_b_kernel(z_ref, fb_ref, y_ref, *, scale):
    fb = fb_ref[...].astype(BF16)
    for p in range(z_ref.shape[1]):
        rhs = jnp.concatenate([z_ref[0, p], z_ref[1, p]], axis=0).astype(BF16)
        y_ref[:, p, :] = jnp.dot(fb, rhs, preferred_element_type=F32) * scale


def _fourier(u, consts):
    S, W = u.shape
    n1, n2, ch, fa, fb, twr, twi = consts
    per = FOURIER_PER_STEP
    z = pl.pallas_call(
        _fourier_a_kernel,
        grid=(n2 // per,),
        in_specs=[pl.BlockSpec((n1, per, W), lambda i: (0, i, 0)), _const_spec(ch.shape),
                  _const_spec(fa.shape), pl.BlockSpec((per, n1, LANES), lambda i: (i, 0, 0)),
                  pl.BlockSpec((per, n1, LANES), lambda i: (i, 0, 0))],
        out_specs=pl.BlockSpec((2, n1, per, W), lambda i: (0, 0, i, 0)),
        out_shape=jax.ShapeDtypeStruct((2, n1, n2, W), F32),
        compiler_params=_cparams(("parallel",)),
        name="fourier_a",
    )(u.reshape(n1, n2, W), ch, fa, twr, twi)
    cg = W // FOURIER_GROUPS
    y = pl.pallas_call(
        functools.partial(_fourier_b_kernel, scale=1.0 / math.sqrt(S * cg)),
        grid=(n1 // per,),
        in_specs=[pl.BlockSpec((2, per, n2, W), lambda i: (0, i, 0, 0)), _const_spec(fb.shape)],
        out_specs=pl.BlockSpec((n2, per, W), lambda i: (0, i, 0)),
        out_shape=jax.ShapeDtypeStruct((n2, n1, W), F32),
        compiler_params=_cparams(("parallel",)),
        name="fourier_b",
    )(z, fb)
    return y.reshape(S, W)


def _attn_kernel(q_ref, k_ref, kp_ref, kn_ref, v_ref, vp_ref, vn_ref, o_ref, lse_ref,
                 kext_ref, vaug_ref):
    R = WINDOW_RADIUS
    rows = q_ref.shape[0]
    QB = ATTN_QBLK
    KW = QB + 2 * R
    t = pl.program_id(1)
    nt = pl.num_programs(1)

    lane = lax.broadcasted_iota(jnp.int32, (QB, LANES), 1)
    head0 = lane < HEAD_DIM

    kext_ref[0:R] = kp_ref[...]
    kext_ref[R:R + rows] = k_ref[...]
    kext_ref[R + rows:] = kn_ref[...]
    for h in range(2):
        for lo, hi, src in ((0, R, vp_ref), (R, R + rows, v_ref), (R + rows, rows + 2 * R, vn_ref)):
            val = src[...]
            in_head0 = lax.broadcasted_iota(jnp.int32, val.shape, 1) < HEAD_DIM
            mine = in_head0 if h == 0 else ~in_head0
            vaug_ref[h, lo:hi, 0:LANES] = jnp.where(mine, val, jnp.zeros_like(val))
            vaug_ref[h, lo:hi, LANES:] = mine.astype(BF16)

    r_i = lax.broadcasted_iota(jnp.int32, (QB, KW), 0)
    c_i = lax.broadcasted_iota(jnp.int32, (QB, KW), 1)
    band = jnp.where((c_i >= r_i) & (c_i <= r_i + 2 * R), 0.0, NEG_BIG)
    first_bias = band + jnp.where((t == 0) & (c_i < R), NEG_BIG, 0.0)
    last_bias = band + jnp.where((t == nt - 1) & (c_i >= QB + R), NEG_BIG, 0.0)
    zero16 = jnp.zeros((QB, LANES), BF16)
    contract_lanes = (((1,), (1,)), ((), ()))

    nblk = rows // QB
    for jb in range(nblk):
        q = q_ref[jb * QB:(jb + 1) * QB]
        kwin = kext_ref[jb * QB:jb * QB + KW]
        bias = first_bias if jb == 0 else (last_bias if jb == nblk - 1 else band)
        q2 = jnp.concatenate([jnp.where(head0, q, zero16), jnp.where(head0, zero16, q)], axis=0)
        s2 = lax.dot_general(q2, kwin, contract_lanes, preferred_element_type=F32)
        tot = None
        maxes = []
        for h in range(2):
            s = s2[h * QB:(h + 1) * QB] + bias
            m = jnp.max(s, axis=-1, keepdims=True)
            p = jnp.exp(s - m).astype(BF16)
            part = jnp.dot(p, vaug_ref[h, jb * QB:jb * QB + KW], preferred_element_type=F32)
            tot = part if tot is None else tot + part
            maxes.append(m)
        l = tot[:, LANES:]
        o_ref[jb * QB:(jb + 1) * QB] = (tot[:, :LANES] / l).astype(BF16)
        lse_ref[jb * QB:(jb + 1) * QB] = jnp.where(head0, maxes[0], maxes[1]) + jnp.log(l)


def _attn_pattern(q, k, v):
    P, d, L, _ = q.shape
    rows = min(ATTN_TILE, L)
    assert L % rows == 0 and rows % ATTN_QBLK == 0 and rows // ATTN_QBLK >= 2
    R = WINDOW_RADIUS
    halo_per_tile = rows // R
    n_halo = L // R
    flat = lambda a: a.reshape(P * d, L, LANES)
    main = pl.BlockSpec((None, rows, LANES), lambda b, t: (b, t, 0))
    prev = pl.BlockSpec((None, R, LANES),
                        lambda b, t: (b, jnp.maximum(t * halo_per_tile - 1, 0), 0))
    nxt = pl.BlockSpec((None, R, LANES),
                       lambda b, t: (b, jnp.minimum((t + 1) * halo_per_tile, n_halo - 1), 0))
    o, lse = pl.pallas_call(
        _attn_kernel,
        grid=(P * d, L // rows),
        in_specs=[main, main, prev, nxt, main, prev, nxt],
        out_specs=[main, main],
        out_shape=[jax.ShapeDtypeStruct((P * d, L, LANES), BF16),
                   jax.ShapeDtypeStruct((P * d, L, LANES), F32)],
        scratch_shapes=[pltpu.VMEM((rows + 2 * R, LANES), BF16),
                        pltpu.VMEM((2, rows + 2 * R, 2 * LANES), BF16)],
        compiler_params=_cparams(("parallel", "parallel")),
        name=f"dilated_attn_d{d}",
    )(flat(q), flat(k), flat(k), flat(k), flat(v), flat(v), flat(v))
    return o.reshape(P, d, L, LANES), lse.reshape(P, d, L, LANES)


def _rms(x, w):
    ms = jnp.mean(x * x, axis=-1, keepdims=True)
    return x * lax.rsqrt(ms + RMS_EPS) * w


def _mix_kernel(x_ref, of_ref, ob_ref, g_ref, y_ref, *rest):
    n_attn = 2 * len(DILATIONS)
    attn_refs, wo_ref, out_ref = rest[:n_attn], rest[n_attn], rest[n_attn + 1]
    stage_refs = rest[n_attn + 2:]
    tm = x_ref.shape[0]
    o = of_ref[...] + ob_ref[...]
    W = o.shape[1]
    r_h = lax.broadcasted_iota(jnp.int32, (W, W), 0) // HEAD_DIM
    c_h = lax.broadcasted_iota(jnp.int32, (W, W), 1) // HEAD_DIM
    head_sum = (r_h == c_h).astype(BF16)
    sq_hi, sq_mid, _ = _split3(o * o)
    ms = (jnp.dot(sq_hi, head_sum, preferred_element_type=F32)
          + jnp.dot(sq_mid, head_sum, preferred_element_type=F32)) * (1.0 / HEAD_DIM)
    g = g_ref[...]
    o_a = o * lax.rsqrt(ms + RMS_EPS) * (g * _sigmoid(g))

    acc = x_ref[...]
    acc += jnp.dot(o_a.astype(BF16), wo_ref[0:W], preferred_element_type=F32)
    acc += jnp.dot(y_ref[...].astype(BF16), wo_ref[W:2 * W], preferred_element_type=F32)

    def natural(ref, hp, i):
        cur = [ref[hp, r].astype(F32) for r in range(DILATIONS[i])]
        for lvl in range(i, 0, -1):
            dp, d = DILATIONS[lvl - 1], DILATIONS[lvl]
            ratio = d // dp
            for rp in range(dp):
                for r2 in range(ratio):
                    stage_refs[lvl - 1][rp, pl.ds(r2, tm // d, stride=ratio), :] = cur[r2 * dp + rp]
            cur = [stage_refs[lvl - 1][rp] for rp in range(dp)]
        return cur[0]

    mixed = []
    for hp in range(HEAD_PAIRS):
        outs = [natural(attn_refs[2 * i], hp, i) for i in range(len(DILATIONS))]
        lses = [natural(attn_refs[2 * i + 1], hp, i) for i in range(len(DILATIONS))]
        m = functools.reduce(jnp.maximum, lses)
        ws = [jnp.exp(l - m) for l in lses]
        mixed.append((sum(w * ov for w, ov in zip(ws, outs)) / sum(ws)).astype(BF16))
    acc += jnp.dot(jnp.concatenate(mixed, axis=1), wo_ref[2 * W:], preferred_element_type=F32)
    out_ref[...] = acc


def _mix(x, o_f, o_b, hg, y, attn, w_out, layer):
    S = x.shape[0]
    tm = ROW_TILE
    row = lambda w, col=0: pl.BlockSpec((tm, w), lambda i: (i, col))
    attn_specs, attn_args = [], []
    for d, (o, lse) in zip(DILATIONS, attn):
        spec = pl.BlockSpec((HEAD_PAIRS, d, tm // d, LANES), lambda i: (0, 0, i, 0))
        attn_specs += [spec, spec]
        attn_args += [o, lse]
    return pl.pallas_call(
        _mix_kernel,
        grid=(S // tm,),
        in_specs=[row(D_MODEL), row(HGRN_WIDTH), row(HGRN_WIDTH), row(HGRN_WIDTH, 4),
                  row(FOURIER_WIDTH)] + attn_specs
                 + [_layer_spec((D_MODEL, D_MODEL), layer)],
        out_specs=row(D_MODEL),
        out_shape=jax.ShapeDtypeStruct((S, D_MODEL), F32),
        scratch_shapes=[pltpu.VMEM((d, tm // d, LANES), F32) for d in DILATIONS[:-1]],
        compiler_params=_cparams(("parallel",)),
        name="mix_outproj",
    )(x, o_f, o_b, hg, y, *attn_args, w_out)


def _mlp_kernel(x_ref, nw_ref, wu_ref, wd_ref, fw_ref, out_ref, *, final):
    acc = x_ref[...]
    h = _rms(acc, nw_ref[...]).astype(BF16)
    for c in range(0, D_FF, FF_CHUNK):
        u = jnp.dot(h, wu_ref[:, c:c + FF_CHUNK], preferred_element_type=F32)
        u = jnp.square(jnp.maximum(u, 0.0)).astype(BF16)
        acc += jnp.dot(u, wd_ref[c:c + FF_CHUNK], preferred_element_type=F32)
    out_ref[...] = _rms(acc, fw_ref[...]) if final else acc


def _mlp(x, norm_w, w_up, w_down, layer, final_w, final):
    S = x.shape[0]
    tm = ROW_TILE
    row = pl.BlockSpec((tm, D_MODEL), lambda i: (i, 0))
    return pl.pallas_call(
        functools.partial(_mlp_kernel, final=final),
        grid=(S // tm,),
        in_specs=[row, _const_spec((1, D_MODEL)), _layer_spec((D_MODEL, D_FF), layer),
                  _layer_spec((D_FF, D_MODEL), layer), _const_spec((1, D_MODEL))],
        out_specs=row,
        out_shape=jax.ShapeDtypeStruct((S, D_MODEL), F32),
        compiler_params=_cparams(("parallel",)),
        name="mlp",
    )(x, norm_w, w_up, w_down, final_w)


def _rope_tables(positions):
    inv_freq = ROPE_THETA ** (-jnp.arange(0, HEAD_DIM, 2, dtype=F32) / HEAD_DIM)
    ang = positions.astype(F32)[:, None] * inv_freq
    cos, sin = jnp.cos(ang), jnp.sin(ang)
    reps = LANES // HEAD_DIM
    cos_t = jnp.tile(cos, (1, 2 * reps))
    sin_t = jnp.tile(jnp.concatenate([-sin, sin], axis=1), (1, reps))
    return cos_t, sin_t


def kernel(x, positions, attn_norm_w, w_in, hgrn_lower_bounds, w_out, mlp_norm_w, w_up, w_down,
           final_norm_w):
    B, S, D = x.shape
    depth = w_in.shape[0]
    assert B == 1 and D == D_MODEL and S % (2 * ATTN_QBLK * max(DILATIONS)) == 0
    assert S % (HGRN_BLOCK * 2) == 0

    p_lb = jax.nn.softmax(hgrn_lower_bounds.astype(F32), axis=0)
    lb_all = jnp.cumsum(p_lb, axis=0) - p_lb[0:1]
    cos_t, sin_t = _rope_tables(positions[0])
    fconsts = _fourier_constants(S)
    w_in16, w_out16 = w_in.astype(BF16), w_out.astype(BF16)
    w_up16, w_down16 = w_up.astype(BF16), w_down.astype(BF16)
    final_w = final_norm_w.reshape(1, D)

    xs = x[0]
    for layer in range(depth):
        hg, fu, qkv = _inproj(xs, attn_norm_w[layer].reshape(1, D), w_in16, layer, cos_t, sin_t)
        o_f, o_b = _hgrn(hg, lb_all[layer])
        y = _fourier(fu, fconsts)
        attn = [_attn_pattern(q, k, v) for q, k, v in qkv]
        xs = _mix(xs, o_f, o_b, hg, y, attn, w_out16, layer)
        xs = _mlp(xs, mlp_norm_w[layer].reshape(1, D), w_up16, w_down16, layer, final_w,
                  final=(layer == depth - 1))
    return xs[None]
```

```python
import functools
import math

import numpy as np
import jax
import jax.numpy as jnp
from jax import lax
from jax.experimental import pallas as pl
from jax.experimental.pallas import tpu as pltpu

F32 = jnp.float32
BF16 = jnp.bfloat16

D_MODEL = 1024
HEAD_DIM = 64
HGRN_WIDTH = 256
FOURIER_WIDTH = 256
FOURIER_GROUPS = 4
ATTN_WIDTH = 512
ATTN_HEADS = 8
HEAD_PAIRS = ATTN_HEADS // 2
LANES = 128
IN_WIDTH = 5 * HGRN_WIDTH + FOURIER_WIDTH + 3 * ATTN_WIDTH
D_FF = 4 * D_MODEL
DILATIONS = (1, 4, 16)
WINDOW_RADIUS = 64
ROPE_THETA = 10000.0
RMS_EPS = 1e-6
NEG_BIG = -1e30

HGRN_CHUNK = 64
HGRN_BLOCK = 256
ATTN_TILE = 2048
ATTN_QBLK = 128
ROW_TILE = 512
FF_CHUNK = 1024
FOURIER_PER_STEP = 8
VMEM_LIMIT = 56 * 1024 * 1024


def _cparams(sem):
    return pltpu.CompilerParams(dimension_semantics=sem, vmem_limit_bytes=VMEM_LIMIT)


def _const_spec(shape):
    nd = len(shape)
    return pl.BlockSpec(shape, lambda *_: (0,) * nd)


def _layer_spec(shape, layer):
    nd = len(shape)
    return pl.BlockSpec((None,) + tuple(shape), lambda *_: (layer,) + (0,) * nd)


def _sigmoid(x):
    return 1.0 / (1.0 + jnp.exp(-x))


def _inproj_kernel(x_ref, nw_ref, w_ref, cos_ref, sin_ref, hg_ref, fu_ref, *rest):
    n_pat = len(DILATIONS)
    qkv_refs, stage_refs = rest[:3 * n_pat], rest[3 * n_pat:]
    tm = x_ref.shape[0]

    def store_regrouped(val, which, hp):
        stage_refs[0][0] = val
        qkv_refs[which][hp, 0] = val.astype(BF16)
        for i in range(1, n_pat):
            dp, d = DILATIONS[i - 1], DILATIONS[i]
            ratio = d // dp
            for rp in range(dp):
                for r2 in range(ratio):
                    rows = stage_refs[i - 1][rp, pl.ds(r2, tm // d, stride=ratio), :]
                    r = r2 * dp + rp
                    if i + 1 < n_pat:
                        stage_refs[i][r] = rows
                    qkv_refs[3 * i + which][hp, r] = rows.astype(BF16)

    x = x_ref[...]
    ms = jnp.mean(x * x, axis=-1, keepdims=True)
    h = (x * lax.rsqrt(ms + RMS_EPS) * nw_ref[...]).astype(BF16)

    def proj(c0, c1):
        return jnp.dot(h, w_ref[:, c0:c1], preferred_element_type=F32)

    hg_w = 5 * HGRN_WIDTH
    cos = cos_ref[...]
    sin = sin_ref[...]
    lane = lax.broadcasted_iota(jnp.int32, cos.shape, 1)
    low_half = (lane % HEAD_DIM) < (HEAD_DIM // 2)

    def rotary(t):
        partner = jnp.where(low_half, pltpu.roll(t, LANES - HEAD_DIM // 2, 1),
                            pltpu.roll(t, HEAD_DIM // 2, 1))
        return t * cos + partner * sin

    base = hg_w + FOURIER_WIDTH
    for which in range(3):
        for hp0 in range(0, HEAD_PAIRS, 2):
            c = base + which * ATTN_WIDTH + hp0 * LANES
            both = proj(c, c + 2 * LANES)
            for hp in (hp0, hp0 + 1):
                val = both[:, (hp - hp0) * LANES:(hp - hp0 + 1) * LANES]
                if which == 0:
                    val = rotary(val) * (HEAD_DIM ** -0.5)
                elif which == 1:
                    val = rotary(val)
                store_regrouped(val, which, hp)

    for c in range(0, hg_w, HGRN_WIDTH):
        hg_ref[:, c:c + HGRN_WIDTH] = proj(c, c + HGRN_WIDTH)
    fu_ref[...] = proj(hg_w, hg_w + FOURIER_WIDTH)


def _inproj(x, norm_w, w_in, layer, cos_t, sin_t):
    S = x.shape[0]
    tm = ROW_TILE
    row = lambda w: pl.BlockSpec((tm, w), lambda i: (i, 0))
    qkv_specs, qkv_shapes = [], []
    for d in DILATIONS:
        for _ in range(3):
            qkv_specs.append(pl.BlockSpec((HEAD_PAIRS, d, tm // d, LANES), lambda i: (0, 0, i, 0)))
            qkv_shapes.append(jax.ShapeDtypeStruct((HEAD_PAIRS, d, S // d, LANES), BF16))
    outs = pl.pallas_call(
        _inproj_kernel,
        grid=(S // tm,),
        in_specs=[row(D_MODEL), _const_spec((1, D_MODEL)),
                  _layer_spec((D_MODEL, IN_WIDTH), layer),
                  row(LANES), row(LANES)],
        out_specs=[row(5 * HGRN_WIDTH), row(FOURIER_WIDTH)] + qkv_specs,
        out_shape=[jax.ShapeDtypeStruct((S, 5 * HGRN_WIDTH), F32),
                   jax.ShapeDtypeStruct((S, FOURIER_WIDTH), F32)] + qkv_shapes,
        scratch_shapes=[pltpu.VMEM((d, tm // d, LANES), F32) for d in DILATIONS[:-1]],
        compiler_params=_cparams(("parallel",)),
        name="inproj",
    )(x, norm_w, w_in, cos_t, sin_t)
    return outs[0], outs[1], [outs[2 + 3 * i:5 + 3 * i] for i in range(len(DILATIONS))]


def _split3(x):
    hi = x.astype(BF16)
    r = x - hi.astype(F32)
    mid = r.astype(BF16)
    lo = (r - mid.astype(F32)).astype(BF16)
    return hi, mid, lo


def _tanh_sigmoid(x):
    return 0.5 * jnp.tanh(0.5 * x) + 0.5


HGRN_SLAB = 2 * HEAD_DIM


def _hgrn_gates(q_in, z, lb, reverse):
    C = q_in.shape[0]
    f = lb + (1.0 - lb) * _tanh_sigmoid(z)
    kk = 1.0 - f
    q = q_in * _tanh_sigmoid(q_in)
    r_i = lax.broadcasted_iota(jnp.int32, (C, C), 0)
    c_i = lax.broadcasted_iota(jnp.int32, (C, C), 1)
    causal = (c_i >= r_i) if reverse else (c_i <= r_i)
    ones_tri = causal.astype(BF16)
    hi, mid, lo = _split3(jnp.log(f))
    b = (jnp.dot(ones_tri, hi, preferred_element_type=F32)
         + jnp.dot(ones_tri, mid, preferred_element_type=F32)
         + jnp.dot(ones_tri, lo, preferred_element_type=F32))
    return q, kk, b, causal


def _hgrn_decays(q, kk, b, reverse):
    C = q.shape[0]
    edge = 0 if reverse else C - 1
    b_edge = b[edge:edge + 1]
    b_mid = b[C // 2:C // 2 + 1]
    q_loc = q * jnp.exp(b - b_mid)
    k_loc = kk * jnp.exp(b_mid - b)
    q_inter = q_loc * jnp.exp(b_mid)
    k_edge = k_loc * jnp.exp(b_edge - b_mid)
    return (q_inter.astype(BF16), q_loc.astype(BF16), k_loc.astype(BF16), k_edge.astype(BF16),
            jnp.exp(b_edge))


def _hgrn_local(q_loc, k_loc, k_edge, v16, causal):
    C, W = q_loc.shape
    contract_lanes = (((1,), (1,)), ((), ()))
    lane = lax.broadcasted_iota(jnp.int32, (C, W), 1)
    zero16 = jnp.zeros((C, W), BF16)
    scores = []
    v_rows = []
    for h in range(W // HEAD_DIM):
        in_head = (lane // HEAD_DIM) == h
        a = lax.dot_general(jnp.where(in_head, q_loc, zero16), k_loc, contract_lanes,
                            preferred_element_type=F32)
        scores.append(jnp.where(causal, a, 0.0).astype(BF16))
        v_rows.append(jnp.where(in_head, v16, zero16))
    intra = jnp.dot(jnp.concatenate(scores, axis=1), jnp.concatenate(v_rows, axis=0),
                    preferred_element_type=F32)
    s_r = lax.broadcasted_iota(jnp.int32, (HGRN_SLAB, HGRN_SLAB), 0) // HEAD_DIM
    s_c = lax.broadcasted_iota(jnp.int32, (HGRN_SLAB, HGRN_SLAB), 1) // HEAD_DIM
    upd = []
    for c in range(0, W, HGRN_SLAB):
        u = lax.dot_general(v16[:, c:c + HGRN_SLAB], k_edge[:, c:c + HGRN_SLAB],
                            (((0,), (0,)), ((), ())), preferred_element_type=F32)
        upd.append(jnp.where(s_r == s_c, u, 0.0))
    return intra, upd


def _hgrn_kernel(qf_ref, vf_ref, zf_ref, qb_ref, vb_ref, zb_ref, lb_ref, of_ref, ob_ref,
                 sf_ref, sb_ref):
    @pl.when(pl.program_id(0) == 0)
    def _():
        sf_ref[...] = jnp.zeros_like(sf_ref)
        sb_ref[...] = jnp.zeros_like(sb_ref)

    C = HGRN_CHUNK
    n = HGRN_BLOCK // C
    chains = [((qf_ref, vf_ref, zf_ref, of_ref, sf_ref), lb_ref[0:1], False, list(range(n))),
              ((qb_ref, vb_ref, zb_ref, ob_ref, sb_ref), lb_ref[1:2], True,
               list(range(n - 1, -1, -1)))]
    jobs = []
    for refs, lb, reverse, order in chains:
        for j in order:
            rows = slice(j * C, (j + 1) * C)
            jobs.append(dict(refs=refs, rows=rows, reverse=reverse,
                             gates=_hgrn_gates(refs[0][rows], refs[2][rows], lb, reverse)))
    for job in jobs:
        q, kk, b, causal = job["gates"]
        job["decays"] = _hgrn_decays(q, kk, b, job["reverse"])
        job["causal"] = causal
    for job in jobs:
        _, q_loc, k_loc, k_edge, _ = job["decays"]
        v16 = job["refs"][1][job["rows"]].astype(BF16)
        job["local"] = _hgrn_local(q_loc, k_loc, k_edge, v16, job["causal"])

    contract_lanes = (((1,), (1,)), ((), ()))
    W = HGRN_WIDTH
    for refs, _, _, _ in chains:
        state = [refs[4][i] for i in range(W // HGRN_SLAB)]
        for job in jobs:
            if job["refs"] is not refs:
                continue
            q_inter, decay = job["decays"][0], job["decays"][4]
            intra, upd = job["local"]
            inter = [lax.dot_general(q_inter[:, i * HGRN_SLAB:(i + 1) * HGRN_SLAB],
                                     state[i].astype(BF16), contract_lanes,
                                     preferred_element_type=F32) for i in range(len(state))]
            refs[3][job["rows"]] = jnp.concatenate(inter, axis=1) + intra
            state = [state[i] * decay[:, i * HGRN_SLAB:(i + 1) * HGRN_SLAB] + upd[i]
                     for i in range(len(state))]
        for i, s_i in enumerate(state):
            refs[4][i] = s_i


def _hgrn(hg, lb):
    S = hg.shape[0]
    tb = HGRN_BLOCK
    n = S // tb
    W = HGRN_WIDTH
    fwd = lambda col: pl.BlockSpec((tb, W), lambda i: (i, col))
    bwd = lambda col: pl.BlockSpec((tb, W), lambda i: (n - 1 - i, col))
    state = pltpu.VMEM((W // HGRN_SLAB, HGRN_SLAB, HGRN_SLAB), F32)
    return pl.pallas_call(
        _hgrn_kernel,
        grid=(n,),
        in_specs=[fwd(0), fwd(1), fwd(2), bwd(0), bwd(1), bwd(3), _const_spec((2, W))],
        out_specs=[fwd(0), bwd(0)],
        out_shape=[jax.ShapeDtypeStruct((S, W), F32)] * 2,
        scratch_shapes=[state, state],
        compiler_params=_cparams(("arbitrary",)),
        name="hgrn_scan",
    )(hg, hg, hg, hg, hg, hg, lb)


def _fourier_constants(S):
    n1 = 1 << (int(math.log2(S)) // 2)
    n2 = S // n1
    cg = FOURIER_WIDTH // FOURIER_GROUPS
    j = np.arange(cg)
    ang = 2.0 * np.pi * np.outer(j, j) / cg
    eye = np.eye(FOURIER_GROUPS)
    ch = np.concatenate([np.kron(eye, np.cos(ang)), -np.kron(eye, np.sin(ang))], axis=1)
    a1 = 2.0 * np.pi * np.outer(np.arange(n1), np.arange(n1)) / n1
    fr, fi = np.cos(a1), -np.sin(a1)
    fa = np.block([[fr, -fi], [fi, fr]])
    a2 = 2.0 * np.pi * np.outer(np.arange(n2), np.arange(n2)) / n2
    fb = np.concatenate([np.cos(a2), np.sin(a2)], axis=1)
    at = 2.0 * np.pi * np.outer(np.arange(n2), np.arange(n1)) / S
    twr = np.repeat(np.cos(at)[:, :, None], LANES, axis=2)
    twi = np.repeat(-np.sin(at)[:, :, None], LANES, axis=2)
    return (n1, n2, jnp.asarray(ch, F32), jnp.asarray(fa, F32), jnp.asarray(fb, F32),
            jnp.asarray(twr, F32), jnp.asarray(twi, F32))


def _fourier_a_kernel(u_ref, ch_ref, fa_ref, twr_ref, twi_ref, z_ref, u2_ref, z2_ref):
    n1, slabs, W = u_ref.shape
    cols = W // LANES
    ch = ch_ref[...].astype(BF16)
    fa = fa_ref[...].astype(BF16)
    u_blk = u_ref[...]
    for c in range(cols):
        u2_ref[c] = u_blk[:, :, c * LANES:(c + 1) * LANES].reshape(n1 * slabs, LANES)
    for s in range(slabs):
        rows = pl.ds(s, n1, stride=slabs)
        x = jnp.concatenate([u2_ref[c, rows, :] for c in range(cols)], axis=1).astype(BF16)
        uc = jnp.dot(x, ch, preferred_element_type=F32)
        stacked = jnp.concatenate([uc[:, :W], uc[:, W:]], axis=0).astype(BF16)
        a = jnp.dot(fa, stacked, preferred_element_type=F32)
        ar, ai = a[:n1], a[n1:]
        twr = jnp.concatenate([twr_ref[s]] * cols, axis=1)
        twi = jnp.concatenate([twi_ref[s]] * cols, axis=1)
        for part, val in enumerate((ar * twr - ai * twi, ar * twi + ai * twr)):
            for c in range(cols):
                z2_ref[part, c, rows, :] = val[:, c * LANES:(c + 1) * LANES]
    for part in range(2):
        z_ref[part] = jnp.concatenate(
            [z2_ref[part, c].reshape(n1, slabs, LANES) for c in range(cols)], axis=2)


def _fourier_b_kernel(z_ref, fb_ref, y_ref, y2_ref, *, scale):
    n2, count, W = y_ref.shape
    cols = W // LANES
    fb = fb_ref[...].astype(BF16)
    for p in range(count):
        rhs = jnp.concatenate([z_ref[0, p], z_ref[1, p]], axis=0).astype(BF16)
        y = jnp.dot(fb, rhs, preferred_element_type=F32) * scale
        for c in range(cols):
            y2_ref[c, pl.ds(p, n2, stride=count), :] = y[:, c * LANES:(c + 1) * LANES]
    y_ref[...] = jnp.concatenate(
        [y2_ref[c].reshape(n2, count, LANES) for c in range(cols)], axis=2)


def _fourier(u, consts):
    S, W = u.shape
    n1, n2, ch, fa, fb, twr, twi = consts
    per = FOURIER_PER_STEP
    z = pl.pallas_call(
        _fourier_a_kernel,
        grid=(n2 // per,),
        in_specs=[pl.BlockSpec((n1, per, W), lambda i: (0, i, 0)), _const_spec(ch.shape),
                  _const_spec(fa.shape), pl.BlockSpec((per, n1, LANES), lambda i: (i, 0, 0)),
                  pl.BlockSpec((per, n1, LANES), lambda i: (i, 0, 0))],
        out_specs=pl.BlockSpec((2, n1, per, W), lambda i: (0, 0, i, 0)),
        out_shape=jax.ShapeDtypeStruct((2, n1, n2, W), F32),
        scratch_shapes=[pltpu.VMEM((W // LANES, n1 * per, LANES), F32),
                        pltpu.VMEM((2, W // LANES, n1 * per, LANES), F32)],
        compiler_params=_cparams(("parallel",)),
        name="fourier_a",
    )(u.reshape(n1, n2, W), ch, fa, twr, twi)
    cg = W // FOURIER_GROUPS
    y = pl.pallas_call(
        functools.partial(_fourier_b_kernel, scale=1.0 / math.sqrt(S * cg)),
        grid=(n1 // per,),
        in_specs=[pl.BlockSpec((2, per, n2, W), lambda i: (0, i, 0, 0)), _const_spec(fb.shape)],
        out_specs=pl.BlockSpec((n2, per, W), lambda i: (0, i, 0)),
        out_shape=jax.ShapeDtypeStruct((n2, n1, W), F32),
        scratch_shapes=[pltpu.VMEM((W // LANES, n2 * per, LANES), F32)],
        compiler_params=_cparams(("parallel",)),
        name="fourier_b",
    )(z, fb)
    return y.reshape(S, W)


def _attn_kernel(q_ref, k_ref, kp_ref, kn_ref, v_ref, vp_ref, vn_ref, o_ref, lse_ref,
                 kext_ref, vaug_ref):
    R = WINDOW_RADIUS
    n_seq, rows = q_ref.shape[0], q_ref.shape[1]
    QB = ATTN_QBLK
    KW = QB + 2 * R
    t = pl.program_id(1)
    nt = pl.num_programs(1)

    lane = lax.broadcasted_iota(jnp.int32, (QB, LANES), 1)
    head0 = lane < HEAD_DIM
    r_i = lax.broadcasted_iota(jnp.int32, (QB, KW), 0)
    c_i = lax.broadcasted_iota(jnp.int32, (QB, KW), 1)
    band = jnp.where((c_i >= r_i) & (c_i <= r_i + 2 * R), 0.0, NEG_BIG)
    first_bias = band + jnp.where((t == 0) & (c_i < R), NEG_BIG, 0.0)
    last_bias = band + jnp.where((t == nt - 1) & (c_i >= QB + R), NEG_BIG, 0.0)
    zero16 = jnp.zeros((QB, LANES), BF16)
    contract_lanes = (((1,), (1,)), ((), ()))
    nblk = rows // QB

    for b in range(n_seq):
        kext_ref[b, 0:R] = kp_ref[b]
        kext_ref[b, R:R + rows] = k_ref[b]
        kext_ref[b, R + rows:] = kn_ref[b]
        for h in range(2):
            for lo, hi, src in ((0, R, vp_ref), (R, R + rows, v_ref),
                                (R + rows, rows + 2 * R, vn_ref)):
                val = src[b]
                in_head0 = lax.broadcasted_iota(jnp.int32, val.shape, 1) < HEAD_DIM
                mine = in_head0 if h == 0 else ~in_head0
                vaug_ref[b, h, lo:hi, 0:LANES] = jnp.where(mine, val, jnp.zeros_like(val))
                vaug_ref[b, h, lo:hi, LANES:] = mine.astype(BF16)

        for jb in range(nblk):
            q = q_ref[b, jb * QB:(jb + 1) * QB]
            kwin = kext_ref[b, jb * QB:jb * QB + KW]
            bias = first_bias if jb == 0 else (last_bias if jb == nblk - 1 else band)
            q2 = jnp.concatenate([jnp.where(head0, q, zero16), jnp.where(head0, zero16, q)],
                                 axis=0)
            s2 = lax.dot_general(q2, kwin, contract_lanes, preferred_element_type=F32)
            tot = None
            maxes = []
            for h in range(2):
                s = s2[h * QB:(h + 1) * QB] + bias
                m = jnp.max(s, axis=-1, keepdims=True)
                p = jnp.exp(s - m).astype(BF16)
                part = jnp.dot(p, vaug_ref[b, h, jb * QB:jb * QB + KW],
                               preferred_element_type=F32)
                tot = part if tot is None else tot + part
                maxes.append(m)
            l = tot[:, LANES:]
            o_ref[b, jb * QB:(jb + 1) * QB] = (tot[:, :LANES] / l).astype(BF16)
            lse_ref[b, jb * QB:(jb + 1) * QB] = (jnp.where(head0, maxes[0], maxes[1])
                                                 + jnp.log(l))


def _attn_pattern(q, k, v):
    P, d, L, _ = q.shape
    rows = min(ATTN_TILE, L)
    n_seq = ATTN_TILE // rows
    assert L % rows == 0 and rows % ATTN_QBLK == 0 and rows // ATTN_QBLK >= 2
    assert (P * d) % n_seq == 0
    R = WINDOW_RADIUS
    halo_per_tile = rows // R
    n_halo = L // R
    flat = lambda a: a.reshape(P * d, L, LANES)
    main = pl.BlockSpec((n_seq, rows, LANES), lambda b, t: (b, t, 0))
    prev = pl.BlockSpec((n_seq, R, LANES),
                        lambda b, t: (b, jnp.maximum(t * halo_per_tile - 1, 0), 0))
    nxt = pl.BlockSpec((n_seq, R, LANES),
                       lambda b, t: (b, jnp.minimum((t + 1) * halo_per_tile, n_halo - 1), 0))
    o, lse = pl.pallas_call(
        _attn_kernel,
        grid=(P * d // n_seq, L // rows),
        in_specs=[main, main, prev, nxt, main, prev, nxt],
        out_specs=[main, main],
        out_shape=[jax.ShapeDtypeStruct((P * d, L, LANES), BF16),
                   jax.ShapeDtypeStruct((P * d, L, LANES), F32)],
        scratch_shapes=[pltpu.VMEM((n_seq, rows + 2 * R, LANES), BF16),
                        pltpu.VMEM((n_seq, 2, rows + 2 * R, 2 * LANES), BF16)],
        compiler_params=_cparams(("parallel", "parallel")),
        name=f"dilated_attn_d{d}",
    )(flat(q), flat(k), flat(k), flat(k), flat(v), flat(v), flat(v))
    return o.reshape(P, d, L, LANES), lse.reshape(P, d, L, LANES)


def _rms(x, w):
    ms = jnp.mean(x * x, axis=-1, keepdims=True)
    return x * lax.rsqrt(ms + RMS_EPS) * w


def _mix_kernel(x_ref, of_ref, ob_ref, g_ref, y_ref, *rest):
    n_attn = 2 * len(DILATIONS)
    attn_refs, wo_ref, out_ref = rest[:n_attn], rest[n_attn], rest[n_attn + 1]
    stage_refs = rest[n_attn + 2:]
    tm = x_ref.shape[0]
    o = of_ref[...] + ob_ref[...]
    W = o.shape[1]
    r_h = lax.broadcasted_iota(jnp.int32, (W, W), 0) // HEAD_DIM
    c_h = lax.broadcasted_iota(jnp.int32, (W, W), 1) // HEAD_DIM
    head_sum = (r_h == c_h).astype(BF16)
    sq_hi, sq_mid, _ = _split3(o * o)
    ms = (jnp.dot(sq_hi, head_sum, preferred_element_type=F32)
          + jnp.dot(sq_mid, head_sum, preferred_element_type=F32)) * (1.0 / HEAD_DIM)
    g = g_ref[...]
    o_a = o * lax.rsqrt(ms + RMS_EPS) * (g * _sigmoid(g))

    acc = x_ref[...]
    acc += jnp.dot(o_a.astype(BF16), wo_ref[0:W], preferred_element_type=F32)
    acc += jnp.dot(y_ref[...].astype(BF16), wo_ref[W:2 * W], preferred_element_type=F32)

    def natural(ref, hp, i):
        cur = [ref[hp, r].astype(F32) for r in range(DILATIONS[i])]
        for lvl in range(i, 0, -1):
            dp, d = DILATIONS[lvl - 1], DILATIONS[lvl]
            ratio = d // dp
            for rp in range(dp):
                for r2 in range(ratio):
                    stage_refs[lvl - 1][rp, pl.ds(r2, tm // d, stride=ratio), :] = cur[r2 * dp + rp]
            cur = [stage_refs[lvl - 1][rp] for rp in range(dp)]
        return cur[0]

    mixed = []
    for hp in range(HEAD_PAIRS):
        outs = [natural(attn_refs[2 * i], hp, i) for i in range(len(DILATIONS))]
        lses = [natural(attn_refs[2 * i + 1], hp, i) for i in range(len(DILATIONS))]
        m = functools.reduce(jnp.maximum, lses)
        ws = [jnp.exp(l - m) for l in lses]
        mixed.append((sum(w * ov for w, ov in zip(ws, outs)) / sum(ws)).astype(BF16))
    acc += jnp.dot(jnp.concatenate(mixed, axis=1), wo_ref[2 * W:], preferred_element_type=F32)
    out_ref[...] = acc


def _mix(x, o_f, o_b, hg, y, attn, w_out, layer):
    S = x.shape[0]
    tm = ROW_TILE
    row = lambda w, col=0: pl.BlockSpec((tm, w), lambda i: (i, col))
    attn_specs, attn_args = [], []
    for d, (o, lse) in zip(DILATIONS, attn):
        spec = pl.BlockSpec((HEAD_PAIRS, d, tm // d, LANES), lambda i: (0, 0, i, 0))
        attn_specs += [spec, spec]
        attn_args += [o, lse]
    return pl.pallas_call(
        _mix_kernel,
        grid=(S // tm,),
        in_specs=[row(D_MODEL), row(HGRN_WIDTH), row(HGRN_WIDTH), row(HGRN_WIDTH, 4),
                  row(FOURIER_WIDTH)] + attn_specs
                 + [_layer_spec((D_MODEL, D_MODEL), layer)],
        out_specs=row(D_MODEL),
        out_shape=jax.ShapeDtypeStruct((S, D_MODEL), F32),
        scratch_shapes=[pltpu.VMEM((d, tm // d, LANES), F32) for d in DILATIONS[:-1]],
        compiler_params=_cparams(("parallel",)),
        name="mix_outproj",
    )(x, o_f, o_b, hg, y, *attn_args, w_out)


def _mlp_kernel(x_ref, nw_ref, wu_ref, wd_ref, fw_ref, out_ref, *, final):
    acc = x_ref[...]
    h = _rms(acc, nw_ref[...]).astype(BF16)
    for c in range(0, D_FF, FF_CHUNK):
        u = jnp.dot(h, wu_ref[:, c:c + FF_CHUNK], preferred_element_type=F32)
        u = jnp.square(jnp.maximum(u, 0.0)).astype(BF16)
        acc += jnp.dot(u, wd_ref[c:c + FF_CHUNK], preferred_element_type=F32)
    out_ref[...] = _rms(acc, fw_ref[...]) if final else acc


def _mlp(x, norm_w, w_up, w_down, layer, final_w, final):
    S = x.shape[0]
    tm = ROW_TILE
    row = pl.BlockSpec((tm, D_MODEL), lambda i: (i, 0))
    return pl.pallas_call(
        functools.partial(_mlp_kernel, final=final),
        grid=(S // tm,),
        in_specs=[row, _const_spec((1, D_MODEL)), _layer_spec((D_MODEL, D_FF), layer),
                  _layer_spec((D_FF, D_MODEL), layer), _const_spec((1, D_MODEL))],
        out_specs=row,
        out_shape=jax.ShapeDtypeStruct((S, D_MODEL), F32),
        compiler_params=_cparams(("parallel",)),
        name="mlp",
    )(x, norm_w, w_up, w_down, final_w)


def _rope_tables(positions):
    inv_freq = ROPE_THETA ** (-jnp.arange(0, HEAD_DIM, 2, dtype=F32) / HEAD_DIM)
    ang = positions.astype(F32)[:, None] * inv_freq
    cos, sin = jnp.cos(ang), jnp.sin(ang)
    reps = LANES // HEAD_DIM
    cos_t = jnp.tile(cos, (1, 2 * reps))
    sin_t = jnp.tile(jnp.concatenate([-sin, sin], axis=1), (1, reps))
    return cos_t, sin_t


def kernel(x, positions, attn_norm_w, w_in, hgrn_lower_bounds, w_out, mlp_norm_w, w_up, w_down,
           final_norm_w):
    B, S, D = x.shape
    depth = w_in.shape[0]
    assert B == 1 and D == D_MODEL and S % (2 * ATTN_QBLK * max(DILATIONS)) == 0
    assert S % (HGRN_BLOCK * 2) == 0

    p_lb = jax.nn.softmax(hgrn_lower_bounds.astype(F32), axis=0)
    lb_all = jnp.cumsum(p_lb, axis=0) - p_lb[0:1]
    cos_t, sin_t = _rope_tables(positions[0])
    fconsts = _fourier_constants(S)
    w_in16, w_out16 = w_in.astype(BF16), w_out.astype(BF16)
    w_up16, w_down16 = w_up.astype(BF16), w_down.astype(BF16)
    final_w = final_norm_w.reshape(1, D)

    xs = x[0]
    for layer in range(depth):
        hg, fu, qkv = _inproj(xs, attn_norm_w[layer].reshape(1, D), w_in16, layer, cos_t, sin_t)
        o_f, o_b = _hgrn(hg, lb_all[layer])
        y = _fourier(fu, fconsts)
        attn = [_attn_pattern(q, k, v) for q, k, v in qkv]
        xs = _mix(xs, o_f, o_b, hg, y, attn, w_out16, layer)
        xs = _mlp(xs, mlp_norm_w[layer].reshape(1, D), w_up16, w_down16, layer, final_w,
                  final=(layer == depth - 1))
    return xs[None]
```

```python
import functools
import math

import numpy as np
import jax
import jax.numpy as jnp
from jax import lax
from jax.experimental import pallas as pl
from jax.experimental.pallas import tpu as pltpu

F32 = jnp.float32
BF16 = jnp.bfloat16

D_MODEL = 1024
HEAD_DIM = 64
HGRN_WIDTH = 256
FOURIER_WIDTH = 256
FOURIER_GROUPS = 4
ATTN_WIDTH = 512
ATTN_HEADS = 8
HEAD_PAIRS = ATTN_HEADS // 2
LANES = 128
IN_WIDTH = 5 * HGRN_WIDTH + FOURIER_WIDTH + 3 * ATTN_WIDTH
D_FF = 4 * D_MODEL
DILATIONS = (1, 4, 16)
WINDOW_RADIUS = 64
ROPE_THETA = 10000.0
RMS_EPS = 1e-6
NEG_BIG = -1e30

HGRN_CHUNK = 64
HGRN_BLOCK = 256
ATTN_TILE = 2048
ATTN_QBLK = 128
ROW_TILE = 512
FF_CHUNK = 1024
FOURIER_PER_STEP = 8
VMEM_LIMIT = 56 * 1024 * 1024


def _cparams(sem):
    return pltpu.CompilerParams(dimension_semantics=sem, vmem_limit_bytes=VMEM_LIMIT)


def _const_spec(shape):
    nd = len(shape)
    return pl.BlockSpec(shape, lambda *_: (0,) * nd)


def _layer_spec(shape, layer):
    nd = len(shape)
    return pl.BlockSpec((None,) + tuple(shape), lambda *_: (layer,) + (0,) * nd)


def _sigmoid(x):
    return 1.0 / (1.0 + jnp.exp(-x))


def _inproj_kernel(x_ref, nw_ref, w_ref, cos_ref, sin_ref, hg_ref, fu_ref, *rest):
    n_pat = len(DILATIONS)
    qkv_refs, stage_refs = rest[:3 * n_pat], rest[3 * n_pat:]
    tm = x_ref.shape[0]

    def store_regrouped(val, which, hp):
        stage_refs[0][0] = val
        qkv_refs[which][hp, 0] = val.astype(BF16)
        for i in range(1, n_pat):
            dp, d = DILATIONS[i - 1], DILATIONS[i]
            ratio = d // dp
            for rp in range(dp):
                for r2 in range(ratio):
                    rows = stage_refs[i - 1][rp, pl.ds(r2, tm // d, stride=ratio), :]
                    r = r2 * dp + rp
                    if i + 1 < n_pat:
                        stage_refs[i][r] = rows
                    qkv_refs[3 * i + which][hp, r] = rows.astype(BF16)

    x = x_ref[...]
    ms = jnp.mean(x * x, axis=-1, keepdims=True)
    h = (x * lax.rsqrt(ms + RMS_EPS) * nw_ref[...]).astype(BF16)

    def proj(c0, c1):
        return jnp.dot(h, w_ref[:, c0:c1], preferred_element_type=F32)

    hg_w = 5 * HGRN_WIDTH
    cos = cos_ref[...]
    sin = sin_ref[...]
    lane = lax.broadcasted_iota(jnp.int32, cos.shape, 1)
    low_half = (lane % HEAD_DIM) < (HEAD_DIM // 2)

    def rotary(t):
        partner = jnp.where(low_half, pltpu.roll(t, LANES - HEAD_DIM // 2, 1),
                            pltpu.roll(t, HEAD_DIM // 2, 1))
        return t * cos + partner * sin

    base = hg_w + FOURIER_WIDTH
    for which in range(3):
        for hp0 in range(0, HEAD_PAIRS, 2):
            c = base + which * ATTN_WIDTH + hp0 * LANES
            both = proj(c, c + 2 * LANES)
            for hp in (hp0, hp0 + 1):
                val = both[:, (hp - hp0) * LANES:(hp - hp0 + 1) * LANES]
                if which == 0:
                    val = rotary(val) * (HEAD_DIM ** -0.5)
                elif which == 1:
                    val = rotary(val)
                store_regrouped(val, which, hp)

    for c in range(0, hg_w, HGRN_WIDTH):
        hg_ref[:, c:c + HGRN_WIDTH] = proj(c, c + HGRN_WIDTH)
    fu_ref[...] = proj(hg_w, hg_w + FOURIER_WIDTH)


def _inproj(x, norm_w, w_in, layer, cos_t, sin_t):
    S = x.shape[0]
    tm = ROW_TILE
    row = lambda w: pl.BlockSpec((tm, w), lambda i: (i, 0))
    qkv_specs, qkv_shapes = [], []
    for d in DILATIONS:
        for _ in range(3):
            qkv_specs.append(pl.BlockSpec((HEAD_PAIRS, d, tm // d, LANES), lambda i: (0, 0, i, 0)))
            qkv_shapes.append(jax.ShapeDtypeStruct((HEAD_PAIRS, d, S // d, LANES), BF16))
    outs = pl.pallas_call(
        _inproj_kernel,
        grid=(S // tm,),
        in_specs=[row(D_MODEL), _const_spec((1, D_MODEL)),
                  _layer_spec((D_MODEL, IN_WIDTH), layer),
                  row(LANES), row(LANES)],
        out_specs=[row(5 * HGRN_WIDTH), row(FOURIER_WIDTH)] + qkv_specs,
        out_shape=[jax.ShapeDtypeStruct((S, 5 * HGRN_WIDTH), F32),
                   jax.ShapeDtypeStruct((S, FOURIER_WIDTH), F32)] + qkv_shapes,
        scratch_shapes=[pltpu.VMEM((d, tm // d, LANES), F32) for d in DILATIONS[:-1]],
        compiler_params=_cparams(("parallel",)),
        name="inproj",
    )(x, norm_w, w_in, cos_t, sin_t)
    return outs[0], outs[1], [outs[2 + 3 * i:5 + 3 * i] for i in range(len(DILATIONS))]


def _split3(x):
    hi = x.astype(BF16)
    r = x - hi.astype(F32)
    mid = r.astype(BF16)
    lo = (r - mid.astype(F32)).astype(BF16)
    return hi, mid, lo


def _tanh_sigmoid(x):
    return 0.5 * jnp.tanh(0.5 * x) + 0.5


HGRN_SLAB = 2 * HEAD_DIM


def _hgrn_gates(q_in, z, lb, reverse):
    C = q_in.shape[0]
    f = lb + (1.0 - lb) * _sigmoid(z)
    kk = 1.0 - f
    q = q_in * _tanh_sigmoid(q_in)
    r_i = lax.broadcasted_iota(jnp.int32, (C, C), 0)
    c_i = lax.broadcasted_iota(jnp.int32, (C, C), 1)
    causal = (c_i >= r_i) if reverse else (c_i <= r_i)
    ones_tri = causal.astype(BF16)
    hi, mid, lo = _split3(jnp.log(f))
    b = (jnp.dot(ones_tri, hi, preferred_element_type=F32)
         + jnp.dot(ones_tri, mid, preferred_element_type=F32)
         + jnp.dot(ones_tri, lo, preferred_element_type=F32))
    return q, kk, b, causal


HGRN_SAFE_RANGE = 60.0


def _hgrn_decays(q, kk, b, reverse):
    C = q.shape[0]
    edge = 0 if reverse else C - 1
    b_edge = b[edge:edge + 1]
    b_mid = b[C // 2:C // 2 + 1]
    q_loc = q * jnp.exp(b - b_mid)
    k_loc = kk * jnp.exp(b_mid - b)
    q_inter = q * jnp.exp(b)
    k_edge = kk * jnp.exp(b_edge - b)
    spread = jnp.maximum(jnp.abs(b[0:1] - b_mid), jnp.abs(b[C - 1:C] - b_mid))
    return (q_inter.astype(BF16), q_loc.astype(BF16), k_loc.astype(BF16), k_edge.astype(BF16),
            jnp.exp(b_edge), spread)


def _hgrn_local(q_loc, k_loc, k_edge, v16, causal):
    C, W = q_loc.shape
    contract_lanes = (((1,), (1,)), ((), ()))
    lane = lax.broadcasted_iota(jnp.int32, (C, W), 1)
    zero16 = jnp.zeros((C, W), BF16)
    scores = []
    v_rows = []
    for h in range(W // HEAD_DIM):
        in_head = (lane // HEAD_DIM) == h
        a = lax.dot_general(jnp.where(in_head, q_loc, zero16), k_loc, contract_lanes,
                            preferred_element_type=F32)
        scores.append(jnp.where(causal, a, 0.0).astype(BF16))
        v_rows.append(jnp.where(in_head, v16, zero16))
    intra = jnp.dot(jnp.concatenate(scores, axis=1), jnp.concatenate(v_rows, axis=0),
                    preferred_element_type=F32)
    s_r = lax.broadcasted_iota(jnp.int32, (HGRN_SLAB, HGRN_SLAB), 0) // HEAD_DIM
    s_c = lax.broadcasted_iota(jnp.int32, (HGRN_SLAB, HGRN_SLAB), 1) // HEAD_DIM
    upd = []
    for c in range(0, W, HGRN_SLAB):
        u = lax.dot_general(v16[:, c:c + HGRN_SLAB], k_edge[:, c:c + HGRN_SLAB],
                            (((0,), (0,)), ((), ())), preferred_element_type=F32)
        upd.append(jnp.where(s_r == s_c, u, 0.0))
    return intra, upd


def _head_sums(x):
    lane = lax.broadcasted_iota(jnp.int32, (x.shape[0], LANES), 1)
    cols = [x[:, c:c + LANES] for c in range(0, x.shape[1], LANES)]
    step = HEAD_DIM // 2
    while step >= 1:
        low = (lane // step) % 2 == 0
        cols = [xc + jnp.where(low, pltpu.roll(xc, LANES - step, 1), pltpu.roll(xc, step, 1))
                for xc in cols]
        step //= 2
    return jnp.concatenate(cols, axis=1)


def _hgrn_local_pairwise(q, kk, b, v, reverse, row_refs):
    C, W = q.shape
    b_ref, k_ref, v_ref = row_refs
    b_ref[...] = b
    k_ref[...] = kk
    v_ref[...] = v
    t_i = lax.broadcasted_iota(jnp.int32, (C, W), 0)

    def body(s, acc):
        row = pl.ds(s, 1)
        w = q * jnp.exp(jnp.minimum(b - b_ref[row, :], 0.0)) * k_ref[row, :]
        valid = (t_i <= s) if reverse else (t_i >= s)
        return acc + jnp.where(valid, _head_sums(w), 0.0) * v_ref[row, :]

    return lax.fori_loop(0, C, body, jnp.zeros((C, W), F32))


def _hgrn_kernel(qf_ref, vf_ref, zf_ref, qb_ref, vb_ref, zb_ref, lb_ref, of_ref, ob_ref,
                 sf_ref, sb_ref, local_ref, rowb_ref, rowk_ref, rowv_ref):
    @pl.when(pl.program_id(0) == 0)
    def _():
        sf_ref[...] = jnp.zeros_like(sf_ref)
        sb_ref[...] = jnp.zeros_like(sb_ref)

    C = HGRN_CHUNK
    n = HGRN_BLOCK // C
    chains = [((qf_ref, vf_ref, zf_ref, of_ref, sf_ref), lb_ref[0:1], False, list(range(n))),
              ((qb_ref, vb_ref, zb_ref, ob_ref, sb_ref), lb_ref[1:2], True,
               list(range(n - 1, -1, -1)))]
    jobs = []
    for refs, lb, reverse, order in chains:
        for j in order:
            rows = slice(j * C, (j + 1) * C)
            jobs.append(dict(refs=refs, rows=rows, reverse=reverse,
                             gates=_hgrn_gates(refs[0][rows], refs[2][rows], lb, reverse)))
    for job in jobs:
        q, kk, b, causal = job["gates"]
        job["decays"] = _hgrn_decays(q, kk, b, job["reverse"])
        job["causal"] = causal
    for idx, job in enumerate(jobs):
        _, q_loc, k_loc, k_edge, _, _ = job["decays"]
        v16 = job["refs"][1][job["rows"]].astype(BF16)
        local_ref[idx], job["upd"] = _hgrn_local(q_loc, k_loc, k_edge, v16, job["causal"])

    spread = functools.reduce(jnp.maximum, [job["decays"][5] for job in jobs])

    @pl.when(jnp.max(spread) > HGRN_SAFE_RANGE)
    def _():
        for idx, job in enumerate(jobs):
            q, kk, b, _ = job["gates"]
            v = job["refs"][1][job["rows"]]
            local_ref[idx] = _hgrn_local_pairwise(q, kk, b, v, job["reverse"],
                                                  (rowb_ref, rowk_ref, rowv_ref))

    contract_lanes = (((1,), (1,)), ((), ()))
    n_slab = HGRN_WIDTH // HGRN_SLAB
    for refs, _, _, _ in chains:
        state = [refs[4][i] for i in range(n_slab)]
        for idx, job in enumerate(jobs):
            if job["refs"] is not refs:
                continue
            q_inter, decay = job["decays"][0], job["decays"][4]
            inter = [lax.dot_general(q_inter[:, i * HGRN_SLAB:(i + 1) * HGRN_SLAB],
                                     state[i].astype(BF16), contract_lanes,
                                     preferred_element_type=F32) for i in range(n_slab)]
            refs[3][job["rows"]] = jnp.concatenate(inter, axis=1) + local_ref[idx]
            state = [state[i] * decay[:, i * HGRN_SLAB:(i + 1) * HGRN_SLAB] + job["upd"][i]
                     for i in range(n_slab)]
        for i, s_i in enumerate(state):
            refs[4][i] = s_i


def _hgrn(hg, lb):
    S = hg.shape[0]
    tb = HGRN_BLOCK
    n = S // tb
    W = HGRN_WIDTH
    fwd = lambda col: pl.BlockSpec((tb, W), lambda i: (i, col))
    bwd = lambda col: pl.BlockSpec((tb, W), lambda i: (n - 1 - i, col))
    state = pltpu.VMEM((W // HGRN_SLAB, HGRN_SLAB, HGRN_SLAB), F32)
    return pl.pallas_call(
        _hgrn_kernel,
        grid=(n,),
        in_specs=[fwd(0), fwd(1), fwd(2), bwd(0), bwd(1), bwd(3), _const_spec((2, W))],
        out_specs=[fwd(0), bwd(0)],
        out_shape=[jax.ShapeDtypeStruct((S, W), F32)] * 2,
        scratch_shapes=[state, state, pltpu.VMEM((2 * tb // HGRN_CHUNK, HGRN_CHUNK, W), F32)]
                       + [pltpu.VMEM((HGRN_CHUNK, W), F32)] * 3,
        compiler_params=_cparams(("arbitrary",)),
        name="hgrn_scan",
    )(hg, hg, hg, hg, hg, hg, lb)


def _fourier_constants(S):
    n1 = 1 << (int(math.log2(S)) // 2)
    n2 = S // n1
    cg = FOURIER_WIDTH // FOURIER_GROUPS
    j = np.arange(cg)
    ang = 2.0 * np.pi * np.outer(j, j) / cg
    eye = np.eye(FOURIER_GROUPS)
    ch = np.concatenate([np.kron(eye, np.cos(ang)), -np.kron(eye, np.sin(ang))], axis=1)
    a1 = 2.0 * np.pi * np.outer(np.arange(n1), np.arange(n1)) / n1
    fr, fi = np.cos(a1), -np.sin(a1)
    fa = np.block([[fr, -fi], [fi, fr]])
    a2 = 2.0 * np.pi * np.outer(np.arange(n2), np.arange(n2)) / n2
    fb = np.concatenate([np.cos(a2), np.sin(a2)], axis=1)
    at = 2.0 * np.pi * np.outer(np.arange(n2), np.arange(n1)) / S
    twr = np.repeat(np.cos(at)[:, :, None], LANES, axis=2)
    twi = np.repeat(-np.sin(at)[:, :, None], LANES, axis=2)
    return (n1, n2, jnp.asarray(ch, F32), jnp.asarray(fa, F32), jnp.asarray(fb, F32),
            jnp.asarray(twr, F32), jnp.asarray(twi, F32))


def _fourier_a_kernel(u_ref, ch_ref, fa_ref, twr_ref, twi_ref, z_ref, u2_ref, z2_ref):
    n1, slabs, W = u_ref.shape
    cols = W // LANES
    ch = ch_ref[...].astype(BF16)
    fa = fa_ref[...].astype(BF16)
    u_blk = u_ref[...]
    for c in range(cols):
        u2_ref[c] = u_blk[:, :, c * LANES:(c + 1) * LANES].reshape(n1 * slabs, LANES)
    for s in range(slabs):
        rows = pl.ds(s, n1, stride=slabs)
        x = jnp.concatenate([u2_ref[c, rows, :] for c in range(cols)], axis=1).astype(BF16)
        uc = jnp.dot(x, ch, preferred_element_type=F32)
        stacked = jnp.concatenate([uc[:, :W], uc[:, W:]], axis=0).astype(BF16)
        a = jnp.dot(fa, stacked, preferred_element_type=F32)
        ar, ai = a[:n1], a[n1:]
        twr = jnp.concatenate([twr_ref[s]] * cols, axis=1)
        twi = jnp.concatenate([twi_ref[s]] * cols, axis=1)
        for part, val in enumerate((ar * twr - ai * twi, ar * twi + ai * twr)):
            for c in range(cols):
                z2_ref[part, c, rows, :] = val[:, c * LANES:(c + 1) * LANES]
    for part in range(2):
        z_ref[part] = jnp.concatenate(
            [z2_ref[part, c].reshape(n1, slabs, LANES) for c in range(cols)], axis=2)


def _fourier_b_kernel(z_ref, fb_ref, y_ref, y2_ref, *, scale):
    n2, count, W = y_ref.shape
    cols = W // LANES
    fb = fb_ref[...].astype(BF16)
    for p in range(count):
        rhs = jnp.concatenate([z_ref[0, p], z_ref[1, p]], axis=0).astype(BF16)
        y = jnp.dot(fb, rhs, preferred_element_type=F32) * scale
        for c in range(cols):
            y2_ref[c, pl.ds(p, n2, stride=count), :] = y[:, c * LANES:(c + 1) * LANES]
    y_ref[...] = jnp.concatenate(
        [y2_ref[c].reshape(n2, count, LANES) for c in range(cols)], axis=2)


def _fourier(u, consts):
    S, W = u.shape
    n1, n2, ch, fa, fb, twr, twi = consts
    per = FOURIER_PER_STEP
    z = pl.pallas_call(
        _fourier_a_kernel,
        grid=(n2 // per,),
        in_specs=[pl.BlockSpec((n1, per, W), lambda i: (0, i, 0)), _const_spec(ch.shape),
                  _const_spec(fa.shape), pl.BlockSpec((per, n1, LANES), lambda i: (i, 0, 0)),
                  pl.BlockSpec((per, n1, LANES), lambda i: (i, 0, 0))],
        out_specs=pl.BlockSpec((2, n1, per, W), lambda i: (0, 0, i, 0)),
        out_shape=jax.ShapeDtypeStruct((2, n1, n2, W), F32),
        scratch_shapes=[pltpu.VMEM((W // LANES, n1 * per, LANES), F32),
                        pltpu.VMEM((2, W // LANES, n1 * per, LANES), F32)],
        compiler_params=_cparams(("parallel",)),
        name="fourier_a",
    )(u.reshape(n1, n2, W), ch, fa, twr, twi)
    cg = W // FOURIER_GROUPS
    y = pl.pallas_call(
        functools.partial(_fourier_b_kernel, scale=1.0 / math.sqrt(S * cg)),
        grid=(n1 // per,),
        in_specs=[pl.BlockSpec((2, per, n2, W), lambda i: (0, i, 0, 0)), _const_spec(fb.shape)],
        out_specs=pl.BlockSpec((n2, per, W), lambda i: (0, i, 0)),
        out_shape=jax.ShapeDtypeStruct((n2, n1, W), F32),
        scratch_shapes=[pltpu.VMEM((W // LANES, n2 * per, LANES), F32)],
        compiler_params=_cparams(("parallel",)),
        name="fourier_b",
    )(z, fb)
    return y.reshape(S, W)


def _attn_kernel(q_ref, k_ref, kp_ref, kn_ref, v_ref, vp_ref, vn_ref, o_ref, lse_ref,
                 kext_ref, vaug_ref):
    R = WINDOW_RADIUS
    n_seq, rows = q_ref.shape[0], q_ref.shape[1]
    QB = ATTN_QBLK
    KW = QB + 2 * R
    t = pl.program_id(1)
    nt = pl.num_programs(1)

    lane = lax.broadcasted_iota(jnp.int32, (QB, LANES), 1)
    head0 = lane < HEAD_DIM
    r_i = lax.broadcasted_iota(jnp.int32, (QB, KW), 0)
    c_i = lax.broadcasted_iota(jnp.int32, (QB, KW), 1)
    band = jnp.where((c_i >= r_i) & (c_i <= r_i + 2 * R), 0.0, NEG_BIG)
    first_bias = band + jnp.where((t == 0) & (c_i < R), NEG_BIG, 0.0)
    last_bias = band + jnp.where((t == nt - 1) & (c_i >= QB + R), NEG_BIG, 0.0)
    zero16 = jnp.zeros((QB, LANES), BF16)
    contract_lanes = (((1,), (1,)), ((), ()))
    nblk = rows // QB

    for b in range(n_seq):
        kext_ref[b, 0:R] = kp_ref[b]
        kext_ref[b, R:R + rows] = k_ref[b]
        kext_ref[b, R + rows:] = kn_ref[b]
        for h in range(2):
            for lo, hi, src in ((0, R, vp_ref), (R, R + rows, v_ref),
                                (R + rows, rows + 2 * R, vn_ref)):
                val = src[b]
                in_head0 = lax.broadcasted_iota(jnp.int32, val.shape, 1) < HEAD_DIM
                mine = in_head0 if h == 0 else ~in_head0
                vaug_ref[b, h, lo:hi, 0:LANES] = jnp.where(mine, val, jnp.zeros_like(val))
                vaug_ref[b, h, lo:hi, LANES:] = mine.astype(BF16)

        for jb in range(nblk):
            q = q_ref[b, jb * QB:(jb + 1) * QB]
            kwin = kext_ref[b, jb * QB:jb * QB + KW]
            bias = first_bias if jb == 0 else (last_bias if jb == nblk - 1 else band)
            q2 = jnp.concatenate([jnp.where(head0, q, zero16), jnp.where(head0, zero16, q)],
                                 axis=0)
            s2 = lax.dot_general(q2, kwin, contract_lanes, preferred_element_type=F32)
            tot = None
            maxes = []
            for h in range(2):
                s = s2[h * QB:(h + 1) * QB] + bias
                m = jnp.max(s, axis=-1, keepdims=True)
                p = jnp.exp(s - m).astype(BF16)
                part = jnp.dot(p, vaug_ref[b, h, jb * QB:jb * QB + KW],
                               preferred_element_type=F32)
                tot = part if tot is None else tot + part
                maxes.append(m)
            l = tot[:, LANES:]
            o_ref[b, jb * QB:(jb + 1) * QB] = (tot[:, :LANES] / l).astype(BF16)
            lse_ref[b, jb * QB:(jb + 1) * QB] = (jnp.where(head0, maxes[0], maxes[1])
                                                 + jnp.log(l))


def _attn_pattern(q, k, v):
    P, d, L, _ = q.shape
    rows = min(ATTN_TILE, L)
    n_seq = ATTN_TILE // rows
    assert L % rows == 0 and rows % ATTN_QBLK == 0 and rows // ATTN_QBLK >= 2
    assert (P * d) % n_seq == 0
    R = WINDOW_RADIUS
    halo_per_tile = rows // R
    n_halo = L // R
    flat = lambda a: a.reshape(P * d, L, LANES)
    main = pl.BlockSpec((n_seq, rows, LANES), lambda b, t: (b, t, 0))
    prev = pl.BlockSpec((n_seq, R, LANES),
                        lambda b, t: (b, jnp.maximum(t * halo_per_tile - 1, 0), 0))
    nxt = pl.BlockSpec((n_seq, R, LANES),
                       lambda b, t: (b, jnp.minimum((t + 1) * halo_per_tile, n_halo - 1), 0))
    o, lse = pl.pallas_call(
        _attn_kernel,
        grid=(P * d // n_seq, L // rows),
        in_specs=[main, main, prev, nxt, main, prev, nxt],
        out_specs=[main, main],
        out_shape=[jax.ShapeDtypeStruct((P * d, L, LANES), BF16),
                   jax.ShapeDtypeStruct((P * d, L, LANES), F32)],
        scratch_shapes=[pltpu.VMEM((n_seq, rows + 2 * R, LANES), BF16),
                        pltpu.VMEM((n_seq, 2, rows + 2 * R, 2 * LANES), BF16)],
        compiler_params=_cparams(("parallel", "parallel")),
        name=f"dilated_attn_d{d}",
    )(flat(q), flat(k), flat(k), flat(k), flat(v), flat(v), flat(v))
    return o.reshape(P, d, L, LANES), lse.reshape(P, d, L, LANES)


def _rms(x, w):
    ms = jnp.mean(x * x, axis=-1, keepdims=True)
    return x * lax.rsqrt(ms + RMS_EPS) * w


def _mix_kernel(x_ref, of_ref, ob_ref, g_ref, y_ref, *rest):
    n_attn = 2 * len(DILATIONS)
    attn_refs, wo_ref, out_ref = rest[:n_attn], rest[n_attn], rest[n_attn + 1]
    stage_refs = rest[n_attn + 2:]
    tm = x_ref.shape[0]
    o = of_ref[...] + ob_ref[...]
    W = o.shape[1]
    r_h = lax.broadcasted_iota(jnp.int32, (W, W), 0) // HEAD_DIM
    c_h = lax.broadcasted_iota(jnp.int32, (W, W), 1) // HEAD_DIM
    head_sum = (r_h == c_h).astype(BF16)
    sq_hi, sq_mid, _ = _split3(o * o)
    ms = (jnp.dot(sq_hi, head_sum, preferred_element_type=F32)
          + jnp.dot(sq_mid, head_sum, preferred_element_type=F32)) * (1.0 / HEAD_DIM)
    g = g_ref[...]
    o_a = o * lax.rsqrt(ms + RMS_EPS) * (g * _sigmoid(g))

    acc = x_ref[...]
    acc += jnp.dot(o_a.astype(BF16), wo_ref[0:W], preferred_element_type=F32)
    acc += jnp.dot(y_ref[...].astype(BF16), wo_ref[W:2 * W], preferred_element_type=F32)

    def natural(ref, hp, i):
        cur = [ref[hp, r].astype(F32) for r in range(DILATIONS[i])]
        for lvl in range(i, 0, -1):
            dp, d = DILATIONS[lvl - 1], DILATIONS[lvl]
            ratio = d // dp
            for rp in range(dp):
                for r2 in range(ratio):
                    stage_refs[lvl - 1][rp, pl.ds(r2, tm // d, stride=ratio), :] = cur[r2 * dp + rp]
            cur = [stage_refs[lvl - 1][rp] for rp in range(dp)]
        return cur[0]

    mixed = []
    for hp in range(HEAD_PAIRS):
        outs = [natural(attn_refs[2 * i], hp, i) for i in range(len(DILATIONS))]
        lses = [natural(attn_refs[2 * i + 1], hp, i) for i in range(len(DILATIONS))]
        m = functools.reduce(jnp.maximum, lses)
        ws = [jnp.exp(l - m) for l in lses]
        mixed.append((sum(w * ov for w, ov in zip(ws, outs)) / sum(ws)).astype(BF16))
    acc += jnp.dot(jnp.concatenate(mixed, axis=1), wo_ref[2 * W:], preferred_element_type=F32)
    out_ref[...] = acc


def _mix(x, o_f, o_b, hg, y, attn, w_out, layer):
    S = x.shape[0]
    tm = ROW_TILE
    row = lambda w, col=0: pl.BlockSpec((tm, w), lambda i: (i, col))
    attn_specs, attn_args = [], []
    for d, (o, lse) in zip(DILATIONS, attn):
        spec = pl.BlockSpec((HEAD_PAIRS, d, tm // d, LANES), lambda i: (0, 0, i, 0))
        attn_specs += [spec, spec]
        attn_args += [o, lse]
    return pl.pallas_call(
        _mix_kernel,
        grid=(S // tm,),
        in_specs=[row(D_MODEL), row(HGRN_WIDTH), row(HGRN_WIDTH), row(HGRN_WIDTH, 4),
                  row(FOURIER_WIDTH)] + attn_specs
                 + [_layer_spec((D_MODEL, D_MODEL), layer)],
        out_specs=row(D_MODEL),
        out_shape=jax.ShapeDtypeStruct((S, D_MODEL), F32),
        scratch_shapes=[pltpu.VMEM((d, tm // d, LANES), F32) for d in DILATIONS[:-1]],
        compiler_params=_cparams(("parallel",)),
        name="mix_outproj",
    )(x, o_f, o_b, hg, y, *attn_args, w_out)


def _mlp_kernel(x_ref, nw_ref, wu_ref, wd_ref, fw_ref, out_ref, *, final):
    acc = x_ref[...]
    h = _rms(acc, nw_ref[...]).astype(BF16)
    for c in range(0, D_FF, FF_CHUNK):
        u = jnp.dot(h, wu_ref[:, c:c + FF_CHUNK], preferred_element_type=F32)
        u = jnp.square(jnp.maximum(u, 0.0)).astype(BF16)
        acc += jnp.dot(u, wd_ref[c:c + FF_CHUNK], preferred_element_type=F32)
    out_ref[...] = _rms(acc, fw_ref[...]) if final else acc


def _mlp(x, norm_w, w_up, w_down, layer, final_w, final):
    S = x.shape[0]
    tm = ROW_TILE
    row = pl.BlockSpec((tm, D_MODEL), lambda i: (i, 0))
    return pl.pallas_call(
        functools.partial(_mlp_kernel, final=final),
        grid=(S // tm,),
        in_specs=[row, _const_spec((1, D_MODEL)), _layer_spec((D_MODEL, D_FF), layer),
                  _layer_spec((D_FF, D_MODEL), layer), _const_spec((1, D_MODEL))],
        out_specs=row,
        out_shape=jax.ShapeDtypeStruct((S, D_MODEL), F32),
        compiler_params=_cparams(("parallel",)),
        name="mlp",
    )(x, norm_w, w_up, w_down, final_w)


def _rope_tables(positions):
    inv_freq = ROPE_THETA ** (-jnp.arange(0, HEAD_DIM, 2, dtype=F32) / HEAD_DIM)
    ang = positions.astype(F32)[:, None] * inv_freq
    cos, sin = jnp.cos(ang), jnp.sin(ang)
    reps = LANES // HEAD_DIM
    cos_t = jnp.tile(cos, (1, 2 * reps))
    sin_t = jnp.tile(jnp.concatenate([-sin, sin], axis=1), (1, reps))
    return cos_t, sin_t


def kernel(x, positions, attn_norm_w, w_in, hgrn_lower_bounds, w_out, mlp_norm_w, w_up, w_down,
           final_norm_w):
    B, S, D = x.shape
    depth = w_in.shape[0]
    assert B == 1 and D == D_MODEL and S % (2 * ATTN_QBLK * max(DILATIONS)) == 0
    assert S % (HGRN_BLOCK * 2) == 0

    p_lb = jax.nn.softmax(hgrn_lower_bounds.astype(F32), axis=0)
    lb_all = jnp.cumsum(p_lb, axis=0) - p_lb[0:1]
    cos_t, sin_t = _rope_tables(positions[0])
    fconsts = _fourier_constants(S)
    w_in16, w_out16 = w_in.astype(BF16), w_out.astype(BF16)
    w_up16, w_down16 = w_up.astype(BF16), w_down.astype(BF16)
    final_w = final_norm_w.reshape(1, D)

    xs = x[0]
    for layer in range(depth):
        hg, fu, qkv = _inproj(xs, attn_norm_w[layer].reshape(1, D), w_in16, layer, cos_t, sin_t)
        o_f, o_b = _hgrn(hg, lb_all[layer])
        y = _fourier(fu, fconsts)
        attn = [_attn_pattern(q, k, v) for q, k, v in qkv]
        xs = _mix(xs, o_f, o_b, hg, y, attn, w_out16, layer)
        xs = _mlp(xs, mlp_norm_w[layer].reshape(1, D), w_up16, w_down16, layer, final_w,
                  final=(layer == depth - 1))
    return xs[None]
```

```python
import functools
import math

import numpy as np
import jax
import jax.numpy as jnp
from jax import lax
from jax.experimental import pallas as pl
from jax.experimental.pallas import tpu as pltpu

F32 = jnp.float32
BF16 = jnp.bfloat16

D_MODEL = 1024
HEAD_DIM = 64
HGRN_WIDTH = 256
FOURIER_WIDTH = 256
FOURIER_GROUPS = 4
ATTN_WIDTH = 512
ATTN_HEADS = 8
HEAD_PAIRS = ATTN_HEADS // 2
LANES = 128
IN_WIDTH = 5 * HGRN_WIDTH + FOURIER_WIDTH + 3 * ATTN_WIDTH
D_FF = 4 * D_MODEL
DILATIONS = (1, 4, 16)
WINDOW_RADIUS = 64
ROPE_THETA = 10000.0
RMS_EPS = 1e-6
NEG_BIG = -1e30
LOG2_E = 1.4426950408889634
LN_2 = 0.6931471805599453

HGRN_CHUNK = 64
HGRN_BLOCK = 256
ATTN_TILE = 2048
ATTN_QBLK = 128
ROW_TILE = 512
FF_CHUNK = 1024
FOURIER_PER_STEP = 8
VMEM_LIMIT = 56 * 1024 * 1024


def _cparams(sem):
    return pltpu.CompilerParams(dimension_semantics=sem, vmem_limit_bytes=VMEM_LIMIT)


def _const_spec(shape):
    nd = len(shape)
    return pl.BlockSpec(shape, lambda *_: (0,) * nd)


def _layer_spec(shape, layer):
    nd = len(shape)
    return pl.BlockSpec((None,) + tuple(shape), lambda *_: (layer,) + (0,) * nd)


def _sigmoid(x):
    return 1.0 / (1.0 + jnp.exp(-x))


def _inproj_kernel(x_ref, nw_ref, w_ref, cos_ref, sin_ref, hg_ref, iv_ref, g_ref, fu_ref, *rest):
    n_pat = len(DILATIONS)
    qkv_refs, stage_refs = rest[:3 * n_pat], rest[3 * n_pat:]
    tm = x_ref.shape[0]

    def store_regrouped(val, which, hp):
        stage_refs[0][0] = val
        qkv_refs[which][hp, 0] = val.astype(BF16)
        for i in range(1, n_pat):
            dp, d = DILATIONS[i - 1], DILATIONS[i]
            ratio = d // dp
            for rp in range(dp):
                for r2 in range(ratio):
                    rows = stage_refs[i - 1][rp, pl.ds(r2, tm // d, stride=ratio), :]
                    r = r2 * dp + rp
                    if i + 1 < n_pat:
                        stage_refs[i][r] = rows
                    qkv_refs[3 * i + which][hp, r] = rows.astype(BF16)

    x = x_ref[...]
    ms = jnp.mean(x * x, axis=-1, keepdims=True)
    h = (x * lax.rsqrt(ms + RMS_EPS) * nw_ref[...]).astype(BF16)

    def proj(c0, c1):
        return jnp.dot(h, w_ref[:, c0:c1], preferred_element_type=F32)

    hg_w = 5 * HGRN_WIDTH
    cos = cos_ref[...]
    sin = sin_ref[...]
    lane = lax.broadcasted_iota(jnp.int32, cos.shape, 1)
    low_half = (lane % HEAD_DIM) < (HEAD_DIM // 2)

    def rotary(t):
        partner = jnp.where(low_half, pltpu.roll(t, LANES - HEAD_DIM // 2, 1),
                            pltpu.roll(t, HEAD_DIM // 2, 1))
        return t * cos + partner * sin

    base = hg_w + FOURIER_WIDTH
    for which in range(3):
        for hp0 in range(0, HEAD_PAIRS, 2):
            c = base + which * ATTN_WIDTH + hp0 * LANES
            both = proj(c, c + 2 * LANES)
            for hp in (hp0, hp0 + 1):
                val = both[:, (hp - hp0) * LANES:(hp - hp0 + 1) * LANES]
                if which == 0:
                    val = rotary(val) * (HEAD_DIM ** -0.5 * LOG2_E)
                elif which == 1:
                    val = rotary(val)
                store_regrouped(val, which, hp)

    W = HGRN_WIDTH
    for j, src in enumerate((0, 2, 3)):
        hg_ref[:, j * W:(j + 1) * W] = proj(src * W, (src + 1) * W)
    iv_ref[...] = proj(W, 2 * W).astype(BF16)
    g_ref[...] = proj(4 * W, 5 * W).astype(BF16)
    fu_ref[...] = proj(hg_w, hg_w + FOURIER_WIDTH)


def _inproj(x, norm_w, w_in, layer, cos_t, sin_t):
    S = x.shape[0]
    tm = ROW_TILE
    row = lambda w: pl.BlockSpec((tm, w), lambda i: (i, 0))
    qkv_specs, qkv_shapes = [], []
    for d in DILATIONS:
        for _ in range(3):
            qkv_specs.append(pl.BlockSpec((HEAD_PAIRS, d, tm // d, LANES), lambda i: (0, 0, i, 0)))
            qkv_shapes.append(jax.ShapeDtypeStruct((HEAD_PAIRS, d, S // d, LANES), BF16))
    outs = pl.pallas_call(
        _inproj_kernel,
        grid=(S // tm,),
        in_specs=[row(D_MODEL), _const_spec((1, D_MODEL)),
                  _layer_spec((D_MODEL, IN_WIDTH), layer),
                  row(LANES), row(LANES)],
        out_specs=[row(3 * HGRN_WIDTH), row(HGRN_WIDTH), row(HGRN_WIDTH), row(FOURIER_WIDTH)]
                  + qkv_specs,
        out_shape=[jax.ShapeDtypeStruct((S, 3 * HGRN_WIDTH), F32),
                   jax.ShapeDtypeStruct((S, HGRN_WIDTH), BF16),
                   jax.ShapeDtypeStruct((S, HGRN_WIDTH), BF16),
                   jax.ShapeDtypeStruct((S, FOURIER_WIDTH), F32)] + qkv_shapes,
        scratch_shapes=[pltpu.VMEM((d, tm // d, LANES), F32) for d in DILATIONS[:-1]],
        compiler_params=_cparams(("parallel",)),
        name="inproj",
    )(x, norm_w, w_in, cos_t, sin_t)
    return outs[:4], [outs[4 + 3 * i:7 + 3 * i] for i in range(len(DILATIONS))]


def _split3(x):
    hi = x.astype(BF16)
    r = x - hi.astype(F32)
    mid = r.astype(BF16)
    lo = (r - mid.astype(F32)).astype(BF16)
    return hi, mid, lo


def _tanh_sigmoid(x):
    return 0.5 * jnp.tanh(0.5 * x) + 0.5


HGRN_SLAB = 2 * HEAD_DIM


def _hgrn_gates(q_in, z, lb, reverse):
    C = q_in.shape[0]
    f = lb + (1.0 - lb) * _sigmoid(z)
    kk = 1.0 - f
    q = q_in * _tanh_sigmoid(q_in)
    r_i = lax.broadcasted_iota(jnp.int32, (C, C), 0)
    c_i = lax.broadcasted_iota(jnp.int32, (C, C), 1)
    causal = (c_i >= r_i) if reverse else (c_i <= r_i)
    ones_tri = causal.astype(BF16)
    hi, mid, lo = _split3(jnp.log(f))
    b = (jnp.dot(ones_tri, hi, preferred_element_type=F32)
         + jnp.dot(ones_tri, mid, preferred_element_type=F32)
         + jnp.dot(ones_tri, lo, preferred_element_type=F32))
    return q, kk, b, causal


HGRN_SAFE_RANGE = 60.0


def _hgrn_decays(q, kk, b, reverse):
    C = q.shape[0]
    edge = 0 if reverse else C - 1
    b_edge = b[edge:edge + 1]
    b_mid = b[C // 2:C // 2 + 1]
    q_loc = q * jnp.exp(b - b_mid)
    k_loc = kk * jnp.exp(b_mid - b)
    q_inter = q * jnp.exp(b)
    k_edge = kk * jnp.exp(b_edge - b)
    spread = jnp.maximum(jnp.abs(b[0:1] - b_mid), jnp.abs(b[C - 1:C] - b_mid))
    return (q_inter.astype(BF16), q_loc.astype(BF16), k_loc.astype(BF16), k_edge.astype(BF16),
            jnp.exp(b_edge), spread)


def _hgrn_local(q_loc, k_loc, k_edge, v16, causal):
    C, W = q_loc.shape
    contract_lanes = (((1,), (1,)), ((), ()))
    lane = lax.broadcasted_iota(jnp.int32, (C, W), 1)
    zero16 = jnp.zeros((C, W), BF16)
    scores = []
    v_rows = []
    for h in range(W // HEAD_DIM):
        in_head = (lane // HEAD_DIM) == h
        a = lax.dot_general(jnp.where(in_head, q_loc, zero16), k_loc, contract_lanes,
                            preferred_element_type=F32)
        scores.append(jnp.where(causal, a, 0.0).astype(BF16))
        v_rows.append(jnp.where(in_head, v16, zero16))
    intra = jnp.dot(jnp.concatenate(scores, axis=1), jnp.concatenate(v_rows, axis=0),
                    preferred_element_type=F32)
    s_r = lax.broadcasted_iota(jnp.int32, (HGRN_SLAB, HGRN_SLAB), 0) // HEAD_DIM
    s_c = lax.broadcasted_iota(jnp.int32, (HGRN_SLAB, HGRN_SLAB), 1) // HEAD_DIM
    upd = []
    for c in range(0, W, HGRN_SLAB):
        u = lax.dot_general(v16[:, c:c + HGRN_SLAB], k_edge[:, c:c + HGRN_SLAB],
                            (((0,), (0,)), ((), ())), preferred_element_type=F32)
        upd.append(jnp.where(s_r == s_c, u, 0.0))
    return intra, upd


def _head_sums(x):
    lane = lax.broadcasted_iota(jnp.int32, (x.shape[0], LANES), 1)
    cols = [x[:, c:c + LANES] for c in range(0, x.shape[1], LANES)]
    step = HEAD_DIM // 2
    while step >= 1:
        low = (lane // step) % 2 == 0
        cols = [xc + jnp.where(low, pltpu.roll(xc, LANES - step, 1), pltpu.roll(xc, step, 1))
                for xc in cols]
        step //= 2
    return jnp.concatenate(cols, axis=1)


def _hgrn_local_pairwise(q, kk, b, v, reverse, row_refs):
    C, W = q.shape
    b_ref, k_ref, v_ref = row_refs
    b_ref[...] = b
    k_ref[...] = kk
    v_ref[...] = v
    t_i = lax.broadcasted_iota(jnp.int32, (C, W), 0)

    def body(s, acc):
        row = pl.ds(s, 1)
        w = q * jnp.exp(jnp.minimum(b - b_ref[row, :], 0.0)) * k_ref[row, :]
        valid = (t_i <= s) if reverse else (t_i >= s)
        return acc + jnp.where(valid, _head_sums(w), 0.0) * v_ref[row, :]

    return lax.fori_loop(0, C, body, jnp.zeros((C, W), F32))


def _hgrn_kernel(qf_ref, vf_ref, zf_ref, qb_ref, vb_ref, zb_ref, lb_ref, of_ref, ob_ref,
                 sf_ref, sb_ref, local_ref, rowb_ref, rowk_ref, rowv_ref):
    @pl.when(pl.program_id(0) == 0)
    def _():
        sf_ref[...] = jnp.zeros_like(sf_ref)
        sb_ref[...] = jnp.zeros_like(sb_ref)

    C = HGRN_CHUNK
    n = HGRN_BLOCK // C
    chains = [((qf_ref, vf_ref, zf_ref, of_ref, sf_ref), lb_ref[0:1], False, list(range(n))),
              ((qb_ref, vb_ref, zb_ref, ob_ref, sb_ref), lb_ref[1:2], True,
               list(range(n - 1, -1, -1)))]
    jobs = []
    for refs, lb, reverse, order in chains:
        for j in order:
            rows = slice(j * C, (j + 1) * C)
            jobs.append(dict(refs=refs, rows=rows, reverse=reverse,
                             gates=_hgrn_gates(refs[0][rows], refs[2][rows], lb, reverse)))
    for job in jobs:
        q, kk, b, causal = job["gates"]
        job["decays"] = _hgrn_decays(q, kk, b, job["reverse"])
        job["causal"] = causal
    for idx, job in enumerate(jobs):
        _, q_loc, k_loc, k_edge, _, _ = job["decays"]
        v16 = job["refs"][1][job["rows"]]
        local_ref[idx], job["upd"] = _hgrn_local(q_loc, k_loc, k_edge, v16, job["causal"])

    spread = functools.reduce(jnp.maximum, [job["decays"][5] for job in jobs])

    @pl.when(jnp.max(spread) > HGRN_SAFE_RANGE)
    def _():
        for idx, job in enumerate(jobs):
            q, kk, b, _ = job["gates"]
            v = job["refs"][1][job["rows"]].astype(F32)
            local_ref[idx] = _hgrn_local_pairwise(q, kk, b, v, job["reverse"],
                                                  (rowb_ref, rowk_ref, rowv_ref))

    contract_lanes = (((1,), (1,)), ((), ()))
    n_slab = HGRN_WIDTH // HGRN_SLAB
    for refs, _, _, _ in chains:
        state = [refs[4][i] for i in range(n_slab)]
        for idx, job in enumerate(jobs):
            if job["refs"] is not refs:
                continue
            q_inter, decay = job["decays"][0], job["decays"][4]
            inter = [lax.dot_general(q_inter[:, i * HGRN_SLAB:(i + 1) * HGRN_SLAB],
                                     state[i].astype(BF16), contract_lanes,
                                     preferred_element_type=F32) for i in range(n_slab)]
            out = jnp.concatenate(inter, axis=1) + local_ref[idx]
            refs[3][job["rows"]] = out.astype(refs[3].dtype)
            state = [state[i] * decay[:, i * HGRN_SLAB:(i + 1) * HGRN_SLAB] + job["upd"][i]
                     for i in range(n_slab)]
        for i, s_i in enumerate(state):
            refs[4][i] = s_i


def _hgrn(hg, iv, lb):
    S = hg.shape[0]
    tb = HGRN_BLOCK
    n = S // tb
    W = HGRN_WIDTH
    fwd = lambda col: pl.BlockSpec((tb, W), lambda i: (i, col))
    bwd = lambda col: pl.BlockSpec((tb, W), lambda i: (n - 1 - i, col))
    state = pltpu.VMEM((W // HGRN_SLAB, HGRN_SLAB, HGRN_SLAB), F32)
    return pl.pallas_call(
        _hgrn_kernel,
        grid=(n,),
        in_specs=[fwd(0), fwd(0), fwd(1), bwd(0), bwd(0), bwd(2), _const_spec((2, W))],
        out_specs=[fwd(0), bwd(0)],
        out_shape=[jax.ShapeDtypeStruct((S, W), BF16)] * 2,
        scratch_shapes=[state, state, pltpu.VMEM((2 * tb // HGRN_CHUNK, HGRN_CHUNK, W), F32)]
                       + [pltpu.VMEM((HGRN_CHUNK, W), F32)] * 3,
        compiler_params=_cparams(("arbitrary",)),
        name="hgrn_scan",
    )(hg, iv, hg, hg, iv, hg, lb)


def _fourier_constants(S):
    n1 = 1 << (int(math.log2(S)) // 2)
    n2 = S // n1
    cg = FOURIER_WIDTH // FOURIER_GROUPS
    j = np.arange(cg)
    ang = 2.0 * np.pi * np.outer(j, j) / cg
    eye = np.eye(FOURIER_GROUPS)
    ch = np.concatenate([np.kron(eye, np.cos(ang)), -np.kron(eye, np.sin(ang))], axis=1)
    a1 = 2.0 * np.pi * np.outer(np.arange(n1), np.arange(n1)) / n1
    fr, fi = np.cos(a1), -np.sin(a1)
    fa = np.block([[fr, -fi], [fi, fr]])
    a2 = 2.0 * np.pi * np.outer(np.arange(n2), np.arange(n2)) / n2
    fb = np.concatenate([np.cos(a2), np.sin(a2)], axis=1)
    at = 2.0 * np.pi * np.outer(np.arange(n2), np.arange(n1)) / S
    twr = np.repeat(np.cos(at)[:, :, None], LANES, axis=2)
    twi = np.repeat(-np.sin(at)[:, :, None], LANES, axis=2)
    return (n1, n2, jnp.asarray(ch, F32), jnp.asarray(fa, F32), jnp.asarray(fb, F32),
            jnp.asarray(twr, F32), jnp.asarray(twi, F32))


def _fourier_a_kernel(u_ref, ch_ref, fa_ref, twr_ref, twi_ref, z_ref, u2_ref, z2_ref):
    n1, slabs, W = u_ref.shape
    cols = W // LANES
    ch = ch_ref[...].astype(BF16)
    fa = fa_ref[...].astype(BF16)
    u_blk = u_ref[...]
    for c in range(cols):
        u2_ref[c] = u_blk[:, :, c * LANES:(c + 1) * LANES].reshape(n1 * slabs, LANES)
    for s in range(slabs):
        rows = pl.ds(s, n1, stride=slabs)
        x = jnp.concatenate([u2_ref[c, rows, :] for c in range(cols)], axis=1).astype(BF16)
        uc = jnp.dot(x, ch, preferred_element_type=F32)
        stacked = jnp.concatenate([uc[:, :W], uc[:, W:]], axis=0).astype(BF16)
        a = jnp.dot(fa, stacked, preferred_element_type=F32)
        ar, ai = a[:n1], a[n1:]
        twr = jnp.concatenate([twr_ref[s]] * cols, axis=1)
        twi = jnp.concatenate([twi_ref[s]] * cols, axis=1)
        for part, val in enumerate((ar * twr - ai * twi, ar * twi + ai * twr)):
            for c in range(cols):
                z2_ref[part, c, rows, :] = val[:, c * LANES:(c + 1) * LANES]
    for part in range(2):
        z_ref[part] = jnp.concatenate(
            [z2_ref[part, c].reshape(n1, slabs, LANES) for c in range(cols)], axis=2)


def _fourier_b_kernel(z_ref, fb_ref, y_ref, y2_ref, *, scale):
    n2, count, W = y_ref.shape
    cols = W // LANES
    fb = fb_ref[...].astype(BF16)
    for p in range(count):
        rhs = jnp.concatenate([z_ref[0, p], z_ref[1, p]], axis=0).astype(BF16)
        y = jnp.dot(fb, rhs, preferred_element_type=F32) * scale
        for c in range(cols):
            y2_ref[c, pl.ds(p, n2, stride=count), :] = y[:, c * LANES:(c + 1) * LANES]
    y_ref[...] = jnp.concatenate(
        [y2_ref[c].reshape(n2, count, LANES) for c in range(cols)], axis=2).astype(y_ref.dtype)


def _fourier(u, consts):
    S, W = u.shape
    n1, n2, ch, fa, fb, twr, twi = consts
    per = FOURIER_PER_STEP
    z = pl.pallas_call(
        _fourier_a_kernel,
        grid=(n2 // per,),
        in_specs=[pl.BlockSpec((n1, per, W), lambda i: (0, i, 0)), _const_spec(ch.shape),
                  _const_spec(fa.shape), pl.BlockSpec((per, n1, LANES), lambda i: (i, 0, 0)),
                  pl.BlockSpec((per, n1, LANES), lambda i: (i, 0, 0))],
        out_specs=pl.BlockSpec((2, n1, per, W), lambda i: (0, 0, i, 0)),
        out_shape=jax.ShapeDtypeStruct((2, n1, n2, W), F32),
        scratch_shapes=[pltpu.VMEM((W // LANES, n1 * per, LANES), F32),
                        pltpu.VMEM((2, W // LANES, n1 * per, LANES), F32)],
        compiler_params=_cparams(("parallel",)),
        name="fourier_a",
    )(u.reshape(n1, n2, W), ch, fa, twr, twi)
    cg = W // FOURIER_GROUPS
    per_b = 2 * per
    y = pl.pallas_call(
        functools.partial(_fourier_b_kernel, scale=1.0 / math.sqrt(S * cg)),
        grid=(n1 // per_b,),
        in_specs=[pl.BlockSpec((2, per_b, n2, W), lambda i: (0, i, 0, 0)), _const_spec(fb.shape)],
        out_specs=pl.BlockSpec((n2, per_b, W), lambda i: (0, i, 0)),
        out_shape=jax.ShapeDtypeStruct((n2, n1, W), BF16),
        scratch_shapes=[pltpu.VMEM((W // LANES, n2 * per_b, LANES), F32)],
        compiler_params=_cparams(("parallel",)),
        name="fourier_b",
    )(z, fb)
    return y.reshape(S, W)


def _attn_kernel(q_ref, k_ref, kp_ref, kn_ref, v_ref, vp_ref, vn_ref, o_ref, lse_ref,
                 kext_ref, vaug_ref):
    R = WINDOW_RADIUS
    n_seq, rows = q_ref.shape[0], q_ref.shape[1]
    QB = ATTN_QBLK
    KW = QB + 2 * R
    t = pl.program_id(1)
    nt = pl.num_programs(1)

    lane = lax.broadcasted_iota(jnp.int32, (QB, LANES), 1)
    head0 = lane < HEAD_DIM
    r_i = lax.broadcasted_iota(jnp.int32, (QB, KW), 0)
    c_i = lax.broadcasted_iota(jnp.int32, (QB, KW), 1)
    band = jnp.where((c_i >= r_i) & (c_i <= r_i + 2 * R), 0.0, NEG_BIG)
    first_bias = band + jnp.where((t == 0) & (c_i < R), NEG_BIG, 0.0)
    last_bias = band + jnp.where((t == nt - 1) & (c_i >= QB + R), NEG_BIG, 0.0)
    zero16 = jnp.zeros((QB, LANES), BF16)
    contract_lanes = (((1,), (1,)), ((), ()))
    nblk = rows // QB

    for b in range(n_seq):
        kext_ref[b, 0:R] = kp_ref[b]
        kext_ref[b, R:R + rows] = k_ref[b]
        kext_ref[b, R + rows:] = kn_ref[b]
        for h in range(2):
            for lo, hi, src in ((0, R, vp_ref), (R, R + rows, v_ref),
                                (R + rows, rows + 2 * R, vn_ref)):
                val = src[b]
                in_head0 = lax.broadcasted_iota(jnp.int32, val.shape, 1) < HEAD_DIM
                mine = in_head0 if h == 0 else ~in_head0
                vaug_ref[b, h, lo:hi, 0:LANES] = jnp.where(mine, val, jnp.zeros_like(val))
                vaug_ref[b, h, lo:hi, LANES:] = mine.astype(BF16)

        for jb in range(nblk):
            q = q_ref[b, jb * QB:(jb + 1) * QB]
            kwin = kext_ref[b, jb * QB:jb * QB + KW]
            bias = first_bias if jb == 0 else (last_bias if jb == nblk - 1 else band)
            q2 = jnp.concatenate([jnp.where(head0, q, zero16), jnp.where(head0, zero16, q)],
                                 axis=0)
            s2 = lax.dot_general(q2, kwin, contract_lanes, preferred_element_type=F32)
            tot = None
            maxes = []
            for h in range(2):
                s = s2[h * QB:(h + 1) * QB] + bias
                m = jnp.max(s, axis=-1, keepdims=True)
                p = jnp.exp2(s - m).astype(BF16)
                part = jnp.dot(p, vaug_ref[b, h, jb * QB:jb * QB + KW],
                               preferred_element_type=F32)
                tot = part if tot is None else tot + part
                maxes.append(m)
            l = tot[:, LANES:]
            o_ref[b, jb * QB:(jb + 1) * QB] = (tot[:, :LANES] / l).astype(BF16)
            lse_ref[b, jb * QB:(jb + 1) * QB] = (jnp.where(head0, maxes[0], maxes[1]) * LN_2
                                                 + jnp.log(l))


def _attn_pattern(q, k, v):
    P, d, L, _ = q.shape
    rows = min(ATTN_TILE, L)
    n_seq = ATTN_TILE // rows
    assert L % rows == 0 and rows % ATTN_QBLK == 0 and rows // ATTN_QBLK >= 2
    assert (P * d) % n_seq == 0
    R = WINDOW_RADIUS
    halo_per_tile = rows // R
    n_halo = L // R
    flat = lambda a: a.reshape(P * d, L, LANES)
    main = pl.BlockSpec((n_seq, rows, LANES), lambda b, t: (b, t, 0))
    prev = pl.BlockSpec((n_seq, R, LANES),
                        lambda b, t: (b, jnp.maximum(t * halo_per_tile - 1, 0), 0))
    nxt = pl.BlockSpec((n_seq, R, LANES),
                       lambda b, t: (b, jnp.minimum((t + 1) * halo_per_tile, n_halo - 1), 0))
    o, lse = pl.pallas_call(
        _attn_kernel,
        grid=(P * d // n_seq, L // rows),
        in_specs=[main, main, prev, nxt, main, prev, nxt],
        out_specs=[main, main],
        out_shape=[jax.ShapeDtypeStruct((P * d, L, LANES), BF16),
                   jax.ShapeDtypeStruct((P * d, L, LANES), F32)],
        scratch_shapes=[pltpu.VMEM((n_seq, rows + 2 * R, LANES), BF16),
                        pltpu.VMEM((n_seq, 2, rows + 2 * R, 2 * LANES), BF16)],
        compiler_params=_cparams(("parallel", "parallel")),
        name=f"dilated_attn_d{d}",
    )(flat(q), flat(k), flat(k), flat(k), flat(v), flat(v), flat(v))
    return o.reshape(P, d, L, LANES), lse.reshape(P, d, L, LANES)


def _rms(x, w):
    ms = jnp.mean(x * x, axis=-1, keepdims=True)
    return x * lax.rsqrt(ms + RMS_EPS) * w


def _mix_kernel(x_ref, of_ref, ob_ref, g_ref, y_ref, *rest):
    n_attn = 2 * len(DILATIONS)
    attn_refs, wo_ref, out_ref = rest[:n_attn], rest[n_attn], rest[n_attn + 1]
    stage_refs = rest[n_attn + 2:]
    tm = x_ref.shape[0]
    o = of_ref[...].astype(F32) + ob_ref[...].astype(F32)
    W = o.shape[1]
    r_h = lax.broadcasted_iota(jnp.int32, (W, W), 0) // HEAD_DIM
    c_h = lax.broadcasted_iota(jnp.int32, (W, W), 1) // HEAD_DIM
    head_sum = (r_h == c_h).astype(BF16)
    sq_hi, sq_mid, _ = _split3(o * o)
    ms = (jnp.dot(sq_hi, head_sum, preferred_element_type=F32)
          + jnp.dot(sq_mid, head_sum, preferred_element_type=F32)) * (1.0 / HEAD_DIM)
    g = g_ref[...].astype(F32)
    o_a = o * lax.rsqrt(ms + RMS_EPS) * (g * _sigmoid(g))

    acc = x_ref[...]
    acc += jnp.dot(o_a.astype(BF16), wo_ref[0:W], preferred_element_type=F32)
    acc += jnp.dot(y_ref[...], wo_ref[W:2 * W], preferred_element_type=F32)

    def natural(ref, hp, i):
        cur = [ref[hp, r].astype(F32) for r in range(DILATIONS[i])]
        for lvl in range(i, 0, -1):
            dp, d = DILATIONS[lvl - 1], DILATIONS[lvl]
            ratio = d // dp
            for rp in range(dp):
                for r2 in range(ratio):
                    stage_refs[lvl - 1][rp, pl.ds(r2, tm // d, stride=ratio), :] = cur[r2 * dp + rp]
            cur = [stage_refs[lvl - 1][rp] for rp in range(dp)]
        return cur[0]

    mixed = []
    for hp in range(HEAD_PAIRS):
        outs = [natural(attn_refs[2 * i], hp, i) for i in range(len(DILATIONS))]
        lses = [natural(attn_refs[2 * i + 1], hp, i) for i in range(len(DILATIONS))]
        m = functools.reduce(jnp.maximum, lses)
        ws = [jnp.exp(l - m) for l in lses]
        mixed.append((sum(w * ov for w, ov in zip(ws, outs)) / sum(ws)).astype(BF16))
    acc += jnp.dot(jnp.concatenate(mixed, axis=1), wo_ref[2 * W:], preferred_element_type=F32)
    out_ref[...] = acc


def _mix(x, o_f, o_b, g, y, attn, w_out, layer):
    S = x.shape[0]
    tm = ROW_TILE
    row = lambda w, col=0: pl.BlockSpec((tm, w), lambda i: (i, col))
    attn_specs, attn_args = [], []
    for d, (o, lse) in zip(DILATIONS, attn):
        spec = pl.BlockSpec((HEAD_PAIRS, d, tm // d, LANES), lambda i: (0, 0, i, 0))
        attn_specs += [spec, spec]
        attn_args += [o, lse]
    return pl.pallas_call(
        _mix_kernel,
        grid=(S // tm,),
        in_specs=[row(D_MODEL), row(HGRN_WIDTH), row(HGRN_WIDTH), row(HGRN_WIDTH),
                  row(FOURIER_WIDTH)] + attn_specs
                 + [_layer_spec((D_MODEL, D_MODEL), layer)],
        out_specs=row(D_MODEL),
        out_shape=jax.ShapeDtypeStruct((S, D_MODEL), F32),
        scratch_shapes=[pltpu.VMEM((d, tm // d, LANES), F32) for d in DILATIONS[:-1]],
        compiler_params=_cparams(("parallel",)),
        name="mix_outproj",
    )(x, o_f, o_b, g, y, *attn_args, w_out)


def _mlp_kernel(x_ref, nw_ref, wu_ref, wd_ref, fw_ref, out_ref, *, final):
    acc = x_ref[...]
    h = _rms(acc, nw_ref[...]).astype(BF16)
    for c in range(0, D_FF, FF_CHUNK):
        u = jnp.dot(h, wu_ref[:, c:c + FF_CHUNK], preferred_element_type=F32)
        u = jnp.square(jnp.maximum(u, 0.0)).astype(BF16)
        acc += jnp.dot(u, wd_ref[c:c + FF_CHUNK], preferred_element_type=F32)
    out_ref[...] = _rms(acc, fw_ref[...]) if final else acc


def _mlp(x, norm_w, w_up, w_down, layer, final_w, final):
    S = x.shape[0]
    tm = ROW_TILE
    row = pl.BlockSpec((tm, D_MODEL), lambda i: (i, 0))
    return pl.pallas_call(
        functools.partial(_mlp_kernel, final=final),
        grid=(S // tm,),
        in_specs=[row, _const_spec((1, D_MODEL)), _layer_spec((D_MODEL, D_FF), layer),
                  _layer_spec((D_FF, D_MODEL), layer), _const_spec((1, D_MODEL))],
        out_specs=row,
        out_shape=jax.ShapeDtypeStruct((S, D_MODEL), F32),
        compiler_params=_cparams(("parallel",)),
        name="mlp",
    )(x, norm_w, w_up, w_down, final_w)


def _rope_tables(positions):
    inv_freq = ROPE_THETA ** (-jnp.arange(0, HEAD_DIM, 2, dtype=F32) / HEAD_DIM)
    ang = positions.astype(F32)[:, None] * inv_freq
    cos, sin = jnp.cos(ang), jnp.sin(ang)
    reps = LANES // HEAD_DIM
    cos_t = jnp.tile(cos, (1, 2 * reps))
    sin_t = jnp.tile(jnp.concatenate([-sin, sin], axis=1), (1, reps))
    return cos_t, sin_t


def kernel(x, positions, attn_norm_w, w_in, hgrn_lower_bounds, w_out, mlp_norm_w, w_up, w_down,
           final_norm_w):
    B, S, D = x.shape
    depth = w_in.shape[0]
    assert B == 1 and D == D_MODEL and S % (2 * ATTN_QBLK * max(DILATIONS)) == 0
    assert S % (HGRN_BLOCK * 2) == 0

    p_lb = jax.nn.softmax(hgrn_lower_bounds.astype(F32), axis=0)
    lb_all = jnp.cumsum(p_lb, axis=0) - p_lb[0:1]
    cos_t, sin_t = _rope_tables(positions[0])
    fconsts = _fourier_constants(S)
    w_in16, w_out16 = w_in.astype(BF16), w_out.astype(BF16)
    w_up16, w_down16 = w_up.astype(BF16), w_down.astype(BF16)
    final_w = final_norm_w.reshape(1, D)

    xs = x[0]
    for layer in range(depth):
        (hg, iv, g, fu), qkv = _inproj(xs, attn_norm_w[layer].reshape(1, D), w_in16, layer,
                                       cos_t, sin_t)
        o_f, o_b = _hgrn(hg, iv, lb_all[layer])
        y = _fourier(fu, fconsts)
        attn = [_attn_pattern(q, k, v) for q, k, v in qkv]
        xs = _mix(xs, o_f, o_b, g, y, attn, w_out16, layer)
        xs = _mlp(xs, mlp_norm_w[layer].reshape(1, D), w_up16, w_down16, layer, final_w,
                  final=(layer == depth - 1))
    return xs[None]
```

```python
import functools
import math

import numpy as np
import jax
import jax.numpy as jnp
from jax import lax
from jax.experimental import pallas as pl
from jax.experimental.pallas import tpu as pltpu

F32 = jnp.float32
BF16 = jnp.bfloat16

D_MODEL = 1024
HEAD_DIM = 64
HGRN_WIDTH = 256
FOURIER_WIDTH = 256
FOURIER_GROUPS = 4
ATTN_WIDTH = 512
ATTN_HEADS = 8
HEAD_PAIRS = ATTN_HEADS // 2
LANES = 128
IN_WIDTH = 5 * HGRN_WIDTH + FOURIER_WIDTH + 3 * ATTN_WIDTH
D_FF = 4 * D_MODEL
DILATIONS = (1, 4, 16)
WINDOW_RADIUS = 64
ROPE_THETA = 10000.0
RMS_EPS = 1e-6
NEG_BIG = -1e30
LOG2_E = 1.4426950408889634
LN_2 = 0.6931471805599453

HGRN_CHUNK = 64
HGRN_BLOCK = 512
ATTN_TILE = 4096
ATTN_QBLK = 128
ROW_TILE = 512
FF_CHUNK = 1024
FOURIER_PER_STEP = 16
VMEM_LIMIT = 56 * 1024 * 1024


def _cparams(sem):
    return pltpu.CompilerParams(dimension_semantics=sem, vmem_limit_bytes=VMEM_LIMIT)


def _const_spec(shape):
    nd = len(shape)
    return pl.BlockSpec(shape, lambda *_: (0,) * nd)


def _layer_spec(shape, layer):
    nd = len(shape)
    return pl.BlockSpec((None,) + tuple(shape), lambda *_: (layer,) + (0,) * nd)


def _sigmoid(x):
    return 1.0 / (1.0 + jnp.exp(-x))


def _inproj_kernel(x_ref, nw_ref, w_ref, cos_ref, sin_ref, hg_ref, iv_ref, g_ref, fu_ref, *rest):
    n_pat = len(DILATIONS)
    qkv_refs, stage_refs = rest[:3 * n_pat], rest[3 * n_pat:]
    tm = x_ref.shape[0]

    def store_regrouped(val, which, hp):
        stage_refs[0][0] = val
        qkv_refs[which][hp, 0] = val.astype(BF16)
        for i in range(1, n_pat):
            dp, d = DILATIONS[i - 1], DILATIONS[i]
            ratio = d // dp
            for rp in range(dp):
                for r2 in range(ratio):
                    rows = stage_refs[i - 1][rp, pl.ds(r2, tm // d, stride=ratio), :]
                    r = r2 * dp + rp
                    if i + 1 < n_pat:
                        stage_refs[i][r] = rows
                    qkv_refs[3 * i + which][hp, r] = rows.astype(BF16)

    x = x_ref[...]
    ms = jnp.mean(x * x, axis=-1, keepdims=True)
    h = (x * lax.rsqrt(ms + RMS_EPS) * nw_ref[...]).astype(BF16)

    def proj(c0, c1):
        return jnp.dot(h, w_ref[:, c0:c1], preferred_element_type=F32)

    hg_w = 5 * HGRN_WIDTH
    cos = cos_ref[...]
    sin = sin_ref[...]
    lane = lax.broadcasted_iota(jnp.int32, cos.shape, 1)
    low_half = (lane % HEAD_DIM) < (HEAD_DIM // 2)

    def rotary(t):
        partner = jnp.where(low_half, pltpu.roll(t, LANES - HEAD_DIM // 2, 1),
                            pltpu.roll(t, HEAD_DIM // 2, 1))
        return t * cos + partner * sin

    base = hg_w + FOURIER_WIDTH
    for which in range(3):
        for hp0 in range(0, HEAD_PAIRS, 2):
            c = base + which * ATTN_WIDTH + hp0 * LANES
            both = proj(c, c + 2 * LANES)
            for hp in (hp0, hp0 + 1):
                val = both[:, (hp - hp0) * LANES:(hp - hp0 + 1) * LANES]
                if which == 0:
                    val = rotary(val) * (HEAD_DIM ** -0.5 * LOG2_E)
                elif which == 1:
                    val = rotary(val)
                store_regrouped(val, which, hp)

    W = HGRN_WIDTH
    for j, src in enumerate((0, 2, 3)):
        hg_ref[:, j * W:(j + 1) * W] = proj(src * W, (src + 1) * W)
    iv_ref[...] = proj(W, 2 * W).astype(BF16)
    g_ref[...] = proj(4 * W, 5 * W).astype(BF16)
    fu_ref[...] = proj(hg_w, hg_w + FOURIER_WIDTH)


def _inproj(x, norm_w, w_in, layer, cos_t, sin_t):
    S = x.shape[0]
    tm = ROW_TILE
    row = lambda w: pl.BlockSpec((tm, w), lambda i: (i, 0))
    qkv_specs, qkv_shapes = [], []
    for d in DILATIONS:
        for _ in range(3):
            qkv_specs.append(pl.BlockSpec((HEAD_PAIRS, d, tm // d, LANES), lambda i: (0, 0, i, 0)))
            qkv_shapes.append(jax.ShapeDtypeStruct((HEAD_PAIRS, d, S // d, LANES), BF16))
    outs = pl.pallas_call(
        _inproj_kernel,
        grid=(S // tm,),
        in_specs=[row(D_MODEL), _const_spec((1, D_MODEL)),
                  _layer_spec((D_MODEL, IN_WIDTH), layer),
                  row(LANES), row(LANES)],
        out_specs=[row(3 * HGRN_WIDTH), row(HGRN_WIDTH), row(HGRN_WIDTH), row(FOURIER_WIDTH)]
                  + qkv_specs,
        out_shape=[jax.ShapeDtypeStruct((S, 3 * HGRN_WIDTH), F32),
                   jax.ShapeDtypeStruct((S, HGRN_WIDTH), BF16),
                   jax.ShapeDtypeStruct((S, HGRN_WIDTH), BF16),
                   jax.ShapeDtypeStruct((S, FOURIER_WIDTH), F32)] + qkv_shapes,
        scratch_shapes=[pltpu.VMEM((d, tm // d, LANES), F32) for d in DILATIONS[:-1]],
        compiler_params=_cparams(("parallel",)),
        name="inproj",
    )(x, norm_w, w_in, cos_t, sin_t)
    return outs[:4], [outs[4 + 3 * i:7 + 3 * i] for i in range(len(DILATIONS))]


def _split3(x):
    hi = x.astype(BF16)
    r = x - hi.astype(F32)
    mid = r.astype(BF16)
    lo = (r - mid.astype(F32)).astype(BF16)
    return hi, mid, lo


def _tanh_sigmoid(x):
    return 0.5 * jnp.tanh(0.5 * x) + 0.5


HGRN_SLAB = 2 * HEAD_DIM


def _hgrn_gates(q_in, z, lb, reverse):
    C = q_in.shape[0]
    f = lb + (1.0 - lb) * _sigmoid(z)
    kk = 1.0 - f
    q = q_in * _tanh_sigmoid(q_in)
    r_i = lax.broadcasted_iota(jnp.int32, (C, C), 0)
    c_i = lax.broadcasted_iota(jnp.int32, (C, C), 1)
    causal = (c_i >= r_i) if reverse else (c_i <= r_i)
    ones_tri = causal.astype(BF16)
    hi, mid, lo = _split3(jnp.log(f))
    b = (jnp.dot(ones_tri, hi, preferred_element_type=F32)
         + jnp.dot(ones_tri, mid, preferred_element_type=F32)
         + jnp.dot(ones_tri, lo, preferred_element_type=F32))
    return q, kk, b, causal


HGRN_SAFE_RANGE = 60.0


def _hgrn_decays(q, kk, b, reverse):
    C = q.shape[0]
    edge = 0 if reverse else C - 1
    b_edge = b[edge:edge + 1]
    b_mid = b[C // 2:C // 2 + 1]
    q_loc = q * jnp.exp(b - b_mid)
    k_loc = kk * jnp.exp(b_mid - b)
    q_inter = q * jnp.exp(b)
    k_edge = kk * jnp.exp(b_edge - b)
    spread = jnp.maximum(jnp.abs(b[0:1] - b_mid), jnp.abs(b[C - 1:C] - b_mid))
    return (q_inter.astype(BF16), q_loc.astype(BF16), k_loc.astype(BF16), k_edge.astype(BF16),
            jnp.exp(b_edge), spread)


def _hgrn_local(q_loc, k_loc, k_edge, v16, causal):
    C, W = q_loc.shape
    contract_lanes = (((1,), (1,)), ((), ()))
    lane = lax.broadcasted_iota(jnp.int32, (C, W), 1)
    zero16 = jnp.zeros((C, W), BF16)
    scores = []
    v_rows = []
    for h in range(W // HEAD_DIM):
        in_head = (lane // HEAD_DIM) == h
        a = lax.dot_general(jnp.where(in_head, q_loc, zero16), k_loc, contract_lanes,
                            preferred_element_type=F32)
        scores.append(jnp.where(causal, a, 0.0).astype(BF16))
        v_rows.append(jnp.where(in_head, v16, zero16))
    intra = jnp.dot(jnp.concatenate(scores, axis=1), jnp.concatenate(v_rows, axis=0),
                    preferred_element_type=F32)
    s_r = lax.broadcasted_iota(jnp.int32, (HGRN_SLAB, HGRN_SLAB), 0) // HEAD_DIM
    s_c = lax.broadcasted_iota(jnp.int32, (HGRN_SLAB, HGRN_SLAB), 1) // HEAD_DIM
    upd = []
    for c in range(0, W, HGRN_SLAB):
        u = lax.dot_general(v16[:, c:c + HGRN_SLAB], k_edge[:, c:c + HGRN_SLAB],
                            (((0,), (0,)), ((), ())), preferred_element_type=F32)
        upd.append(jnp.where(s_r == s_c, u, 0.0))
    return intra, upd


def _head_sums(x):
    lane = lax.broadcasted_iota(jnp.int32, (x.shape[0], LANES), 1)
    cols = [x[:, c:c + LANES] for c in range(0, x.shape[1], LANES)]
    step = HEAD_DIM // 2
    while step >= 1:
        low = (lane // step) % 2 == 0
        cols = [xc + jnp.where(low, pltpu.roll(xc, LANES - step, 1), pltpu.roll(xc, step, 1))
                for xc in cols]
        step //= 2
    return jnp.concatenate(cols, axis=1)


def _hgrn_local_pairwise(q, kk, b, v, reverse, row_refs):
    C, W = q.shape
    b_ref, k_ref, v_ref = row_refs
    b_ref[...] = b
    k_ref[...] = kk
    v_ref[...] = v
    t_i = lax.broadcasted_iota(jnp.int32, (C, W), 0)

    def body(s, acc):
        row = pl.ds(s, 1)
        w = q * jnp.exp(jnp.minimum(b - b_ref[row, :], 0.0)) * k_ref[row, :]
        valid = (t_i <= s) if reverse else (t_i >= s)
        return acc + jnp.where(valid, _head_sums(w), 0.0) * v_ref[row, :]

    return lax.fori_loop(0, C, body, jnp.zeros((C, W), F32))


def _hgrn_kernel(qf_ref, vf_ref, zf_ref, qb_ref, vb_ref, zb_ref, lb_ref, of_ref, ob_ref,
                 sf_ref, sb_ref, local_ref, rowb_ref, rowk_ref, rowv_ref):
    @pl.when(pl.program_id(0) == 0)
    def _():
        sf_ref[...] = jnp.zeros_like(sf_ref)
        sb_ref[...] = jnp.zeros_like(sb_ref)

    C = HGRN_CHUNK
    n = HGRN_BLOCK // C
    chains = [((qf_ref, vf_ref, zf_ref, of_ref, sf_ref), lb_ref[0:1], False, list(range(n))),
              ((qb_ref, vb_ref, zb_ref, ob_ref, sb_ref), lb_ref[1:2], True,
               list(range(n - 1, -1, -1)))]
    jobs = []
    for refs, lb, reverse, order in chains:
        for j in order:
            rows = slice(j * C, (j + 1) * C)
            jobs.append(dict(refs=refs, rows=rows, reverse=reverse,
                             gates=_hgrn_gates(refs[0][rows], refs[2][rows], lb, reverse)))
    for job in jobs:
        q, kk, b, causal = job["gates"]
        job["decays"] = _hgrn_decays(q, kk, b, job["reverse"])
        job["causal"] = causal
    for idx, job in enumerate(jobs):
        _, q_loc, k_loc, k_edge, _, _ = job["decays"]
        v16 = job["refs"][1][job["rows"]]
        local_ref[idx], job["upd"] = _hgrn_local(q_loc, k_loc, k_edge, v16, job["causal"])

    spread = functools.reduce(jnp.maximum, [job["decays"][5] for job in jobs])

    @pl.when(jnp.max(spread) > HGRN_SAFE_RANGE)
    def _():
        for idx, job in enumerate(jobs):
            q, kk, b, _ = job["gates"]
            v = job["refs"][1][job["rows"]].astype(F32)
            local_ref[idx] = _hgrn_local_pairwise(q, kk, b, v, job["reverse"],
                                                  (rowb_ref, rowk_ref, rowv_ref))

    contract_lanes = (((1,), (1,)), ((), ()))
    n_slab = HGRN_WIDTH // HGRN_SLAB
    for refs, _, _, _ in chains:
        state = [refs[4][i] for i in range(n_slab)]
        for idx, job in enumerate(jobs):
            if job["refs"] is not refs:
                continue
            q_inter, decay = job["decays"][0], job["decays"][4]
            inter = [lax.dot_general(q_inter[:, i * HGRN_SLAB:(i + 1) * HGRN_SLAB],
                                     state[i].astype(BF16), contract_lanes,
                                     preferred_element_type=F32) for i in range(n_slab)]
            out = jnp.concatenate(inter, axis=1) + local_ref[idx]
            refs[3][job["rows"]] = out.astype(refs[3].dtype)
            state = [state[i] * decay[:, i * HGRN_SLAB:(i + 1) * HGRN_SLAB] + job["upd"][i]
                     for i in range(n_slab)]
        for i, s_i in enumerate(state):
            refs[4][i] = s_i


def _hgrn(hg, iv, lb):
    S = hg.shape[0]
    tb = HGRN_BLOCK
    n = S // tb
    W = HGRN_WIDTH
    fwd = lambda col: pl.BlockSpec((tb, W), lambda i: (i, col))
    bwd = lambda col: pl.BlockSpec((tb, W), lambda i: (n - 1 - i, col))
    state = pltpu.VMEM((W // HGRN_SLAB, HGRN_SLAB, HGRN_SLAB), F32)
    return pl.pallas_call(
        _hgrn_kernel,
        grid=(n,),
        in_specs=[fwd(0), fwd(0), fwd(1), bwd(0), bwd(0), bwd(2), _const_spec((2, W))],
        out_specs=[fwd(0), bwd(0)],
        out_shape=[jax.ShapeDtypeStruct((S, W), BF16)] * 2,
        scratch_shapes=[state, state, pltpu.VMEM((2 * tb // HGRN_CHUNK, HGRN_CHUNK, W), F32)]
                       + [pltpu.VMEM((HGRN_CHUNK, W), F32)] * 3,
        compiler_params=_cparams(("arbitrary",)),
        name="hgrn_scan",
    )(hg, iv, hg, hg, iv, hg, lb)


def _fourier_constants(S):
    n1 = 1 << (int(math.log2(S)) // 2)
    n2 = S // n1
    cg = FOURIER_WIDTH // FOURIER_GROUPS
    j = np.arange(cg)
    ang = 2.0 * np.pi * np.outer(j, j) / cg
    eye = np.eye(FOURIER_GROUPS)
    ch = np.concatenate([np.kron(eye, np.cos(ang)), -np.kron(eye, np.sin(ang))], axis=1)
    a1 = 2.0 * np.pi * np.outer(np.arange(n1), np.arange(n1)) / n1
    fr, fi = np.cos(a1), -np.sin(a1)
    fa = np.block([[fr, -fi], [fi, fr]])
    a2 = 2.0 * np.pi * np.outer(np.arange(n2), np.arange(n2)) / n2
    fb = np.concatenate([np.cos(a2), np.sin(a2)], axis=1)
    at = 2.0 * np.pi * np.outer(np.arange(n2), np.arange(n1)) / S
    twr = np.repeat(np.cos(at)[:, :, None], LANES, axis=2)
    twi = np.repeat(-np.sin(at)[:, :, None], LANES, axis=2)
    return (n1, n2, jnp.asarray(ch, F32), jnp.asarray(fa, F32), jnp.asarray(fb, F32),
            jnp.asarray(twr, F32), jnp.asarray(twi, F32))


def _fourier_a_kernel(u_ref, ch_ref, fa_ref, twr_ref, twi_ref, z_ref, u2_ref, z2_ref):
    n1, slabs, W = u_ref.shape
    cols = W // LANES
    ch = ch_ref[...].astype(BF16)
    fa = fa_ref[...].astype(BF16)
    u_blk = u_ref[...]
    for c in range(cols):
        u2_ref[c] = u_blk[:, :, c * LANES:(c + 1) * LANES].reshape(n1 * slabs, LANES)
    for s in range(slabs):
        rows = pl.ds(s, n1, stride=slabs)
        x = jnp.concatenate([u2_ref[c, rows, :] for c in range(cols)], axis=1).astype(BF16)
        uc = jnp.dot(x, ch, preferred_element_type=F32)
        stacked = jnp.concatenate([uc[:, :W], uc[:, W:]], axis=0).astype(BF16)
        a = jnp.dot(fa, stacked, preferred_element_type=F32)
        ar, ai = a[:n1], a[n1:]
        twr = jnp.concatenate([twr_ref[s]] * cols, axis=1)
        twi = jnp.concatenate([twi_ref[s]] * cols, axis=1)
        for part, val in enumerate((ar * twr - ai * twi, ar * twi + ai * twr)):
            for c in range(cols):
                z2_ref[part, c, rows, :] = val[:, c * LANES:(c + 1) * LANES]
    for part in range(2):
        z_ref[part] = jnp.concatenate(
            [z2_ref[part, c].reshape(n1, slabs, LANES) for c in range(cols)],
            axis=2).astype(z_ref.dtype)


def _fourier_b_kernel(z_ref, fb_ref, y_ref, y2_ref, *, scale):
    n2, count, W = y_ref.shape
    cols = W // LANES
    fb = fb_ref[...].astype(BF16)
    for p in range(count):
        rhs = jnp.concatenate([z_ref[0, p], z_ref[1, p]], axis=0)
        y = jnp.dot(fb, rhs, preferred_element_type=F32) * scale
        for c in range(cols):
            y2_ref[c, pl.ds(p, n2, stride=count), :] = y[:, c * LANES:(c + 1) * LANES]
    y_ref[...] = jnp.concatenate(
        [y2_ref[c].reshape(n2, count, LANES) for c in range(cols)], axis=2).astype(y_ref.dtype)


def _fourier(u, consts):
    S, W = u.shape
    n1, n2, ch, fa, fb, twr, twi = consts
    per = FOURIER_PER_STEP
    z = pl.pallas_call(
        _fourier_a_kernel,
        grid=(n2 // per,),
        in_specs=[pl.BlockSpec((n1, per, W), lambda i: (0, i, 0)), _const_spec(ch.shape),
                  _const_spec(fa.shape), pl.BlockSpec((per, n1, LANES), lambda i: (i, 0, 0)),
                  pl.BlockSpec((per, n1, LANES), lambda i: (i, 0, 0))],
        out_specs=pl.BlockSpec((2, n1, per, W), lambda i: (0, 0, i, 0)),
        out_shape=jax.ShapeDtypeStruct((2, n1, n2, W), BF16),
        scratch_shapes=[pltpu.VMEM((W // LANES, n1 * per, LANES), F32),
                        pltpu.VMEM((2, W // LANES, n1 * per, LANES), F32)],
        compiler_params=_cparams(("parallel",)),
        name="fourier_a",
    )(u.reshape(n1, n2, W), ch, fa, twr, twi)
    cg = W // FOURIER_GROUPS
    y = pl.pallas_call(
        functools.partial(_fourier_b_kernel, scale=1.0 / math.sqrt(S * cg)),
        grid=(n1 // per,),
        in_specs=[pl.BlockSpec((2, per, n2, W), lambda i: (0, i, 0, 0)), _const_spec(fb.shape)],
        out_specs=pl.BlockSpec((n2, per, W), lambda i: (0, i, 0)),
        out_shape=jax.ShapeDtypeStruct((n2, n1, W), BF16),
        scratch_shapes=[pltpu.VMEM((W // LANES, n2 * per, LANES), F32)],
        compiler_params=_cparams(("parallel",)),
        name="fourier_b",
    )(z, fb)
    return y.reshape(S, W)


def _attn_kernel(q_ref, k_ref, kp_ref, kn_ref, v_ref, vp_ref, vn_ref, o_ref, lse_ref,
                 kext_ref, vaug_ref):
    R = WINDOW_RADIUS
    n_seq, rows = q_ref.shape[0], q_ref.shape[1]
    QB = ATTN_QBLK
    KW = QB + 2 * R
    t = pl.program_id(1)
    nt = pl.num_programs(1)

    lane = lax.broadcasted_iota(jnp.int32, (QB, LANES), 1)
    head0 = lane < HEAD_DIM
    r_i = lax.broadcasted_iota(jnp.int32, (QB, KW), 0)
    c_i = lax.broadcasted_iota(jnp.int32, (QB, KW), 1)
    band = jnp.where((c_i >= r_i) & (c_i <= r_i + 2 * R), 0.0, NEG_BIG)
    first_bias = band + jnp.where((t == 0) & (c_i < R), NEG_BIG, 0.0)
    last_bias = band + jnp.where((t == nt - 1) & (c_i >= QB + R), NEG_BIG, 0.0)
    zero16 = jnp.zeros((QB, LANES), BF16)
    contract_lanes = (((1,), (1,)), ((), ()))
    nblk = rows // QB

    for b in range(n_seq):
        kext_ref[b, 0:R] = kp_ref[b]
        kext_ref[b, R:R + rows] = k_ref[b]
        kext_ref[b, R + rows:] = kn_ref[b]
        for h in range(2):
            for lo, hi, src in ((0, R, vp_ref), (R, R + rows, v_ref),
                                (R + rows, rows + 2 * R, vn_ref)):
                val = src[b]
                in_head0 = lax.broadcasted_iota(jnp.int32, val.shape, 1) < HEAD_DIM
                mine = in_head0 if h == 0 else ~in_head0
                vaug_ref[b, h, lo:hi, 0:LANES] = jnp.where(mine, val, jnp.zeros_like(val))
                vaug_ref[b, h, lo:hi, LANES:] = mine.astype(BF16)

        for jb in range(nblk):
            q = q_ref[b, jb * QB:(jb + 1) * QB]
            kwin = kext_ref[b, jb * QB:jb * QB + KW]
            bias = first_bias if jb == 0 else (last_bias if jb == nblk - 1 else band)
            q2 = jnp.concatenate([jnp.where(head0, q, zero16), jnp.where(head0, zero16, q)],
                                 axis=0)
            s2 = lax.dot_general(q2, kwin, contract_lanes, preferred_element_type=F32)
            tot = None
            maxes = []
            for h in range(2):
                s = s2[h * QB:(h + 1) * QB] + bias
                m = jnp.max(s, axis=-1, keepdims=True)
                p = jnp.exp2(s - m).astype(BF16)
                part = jnp.dot(p, vaug_ref[b, h, jb * QB:jb * QB + KW],
                               preferred_element_type=F32)
                tot = part if tot is None else tot + part
                maxes.append(m)
            l = tot[:, LANES:]
            o_ref[b, jb * QB:(jb + 1) * QB] = (tot[:, :LANES] / l).astype(BF16)
            lse_ref[b, jb * QB:(jb + 1) * QB] = (jnp.where(head0, maxes[0], maxes[1]) * LN_2
                                                 + jnp.log(l))


def _attn_pattern(q, k, v):
    P, d, L, _ = q.shape
    rows = min(ATTN_TILE, L)
    n_seq = ATTN_TILE // rows
    assert L % rows == 0 and rows % ATTN_QBLK == 0 and rows // ATTN_QBLK >= 2
    assert (P * d) % n_seq == 0
    R = WINDOW_RADIUS
    halo_per_tile = rows // R
    n_halo = L // R
    flat = lambda a: a.reshape(P * d, L, LANES)
    main = pl.BlockSpec((n_seq, rows, LANES), lambda b, t: (b, t, 0))
    prev = pl.BlockSpec((n_seq, R, LANES),
                        lambda b, t: (b, jnp.maximum(t * halo_per_tile - 1, 0), 0))
    nxt = pl.BlockSpec((n_seq, R, LANES),
                       lambda b, t: (b, jnp.minimum((t + 1) * halo_per_tile, n_halo - 1), 0))
    o, lse = pl.pallas_call(
        _attn_kernel,
        grid=(P * d // n_seq, L // rows),
        in_specs=[main, main, prev, nxt, main, prev, nxt],
        out_specs=[main, main],
        out_shape=[jax.ShapeDtypeStruct((P * d, L, LANES), BF16),
                   jax.ShapeDtypeStruct((P * d, L, LANES), F32)],
        scratch_shapes=[pltpu.VMEM((n_seq, rows + 2 * R, LANES), BF16),
                        pltpu.VMEM((n_seq, 2, rows + 2 * R, 2 * LANES), BF16)],
        compiler_params=_cparams(("parallel", "parallel")),
        name=f"dilated_attn_d{d}",
    )(flat(q), flat(k), flat(k), flat(k), flat(v), flat(v), flat(v))
    return o.reshape(P, d, L, LANES), lse.reshape(P, d, L, LANES)


def _rms(x, w):
    ms = jnp.mean(x * x, axis=-1, keepdims=True)
    return x * lax.rsqrt(ms + RMS_EPS) * w


def _mix_kernel(x_ref, of_ref, ob_ref, g_ref, y_ref, *rest):
    n_attn = 2 * len(DILATIONS)
    attn_refs, wo_ref, out_ref = rest[:n_attn], rest[n_attn], rest[n_attn + 1]
    stage_refs = rest[n_attn + 2:]
    tm = x_ref.shape[0]
    o = of_ref[...].astype(F32) + ob_ref[...].astype(F32)
    W = o.shape[1]
    r_h = lax.broadcasted_iota(jnp.int32, (W, W), 0) // HEAD_DIM
    c_h = lax.broadcasted_iota(jnp.int32, (W, W), 1) // HEAD_DIM
    head_sum = (r_h == c_h).astype(BF16)
    sq_hi, sq_mid, _ = _split3(o * o)
    ms = (jnp.dot(sq_hi, head_sum, preferred_element_type=F32)
          + jnp.dot(sq_mid, head_sum, preferred_element_type=F32)) * (1.0 / HEAD_DIM)
    g = g_ref[...].astype(F32)
    o_a = o * lax.rsqrt(ms + RMS_EPS) * (g * _sigmoid(g))

    acc = x_ref[...]
    acc += jnp.dot(o_a.astype(BF16), wo_ref[0:W], preferred_element_type=F32)
    acc += jnp.dot(y_ref[...], wo_ref[W:2 * W], preferred_element_type=F32)

    def natural(ref, hp, i):
        cur = [ref[hp, r].astype(F32) for r in range(DILATIONS[i])]
        for lvl in range(i, 0, -1):
            dp, d = DILATIONS[lvl - 1], DILATIONS[lvl]
            ratio = d // dp
            for rp in range(dp):
                for r2 in range(ratio):
                    stage_refs[lvl - 1][rp, pl.ds(r2, tm // d, stride=ratio), :] = cur[r2 * dp + rp]
            cur = [stage_refs[lvl - 1][rp] for rp in range(dp)]
        return cur[0]

    mixed = []
    for hp in range(HEAD_PAIRS):
        outs = [natural(attn_refs[2 * i], hp, i) for i in range(len(DILATIONS))]
        lses = [natural(attn_refs[2 * i + 1], hp, i) for i in range(len(DILATIONS))]
        m = functools.reduce(jnp.maximum, lses)
        ws = [jnp.exp(l - m) for l in lses]
        mixed.append((sum(w * ov for w, ov in zip(ws, outs)) / sum(ws)).astype(BF16))
    acc += jnp.dot(jnp.concatenate(mixed, axis=1), wo_ref[2 * W:], preferred_element_type=F32)
    out_ref[...] = acc


def _mix(x, o_f, o_b, g, y, attn, w_out, layer):
    S = x.shape[0]
    tm = ROW_TILE
    row = lambda w, col=0: pl.BlockSpec((tm, w), lambda i: (i, col))
    attn_specs, attn_args = [], []
    for d, (o, lse) in zip(DILATIONS, attn):
        spec = pl.BlockSpec((HEAD_PAIRS, d, tm // d, LANES), lambda i: (0, 0, i, 0))
        attn_specs += [spec, spec]
        attn_args += [o, lse]
    return pl.pallas_call(
        _mix_kernel,
        grid=(S // tm,),
        in_specs=[row(D_MODEL), row(HGRN_WIDTH), row(HGRN_WIDTH), row(HGRN_WIDTH),
                  row(FOURIER_WIDTH)] + attn_specs
                 + [_layer_spec((D_MODEL, D_MODEL), layer)],
        out_specs=row(D_MODEL),
        out_shape=jax.ShapeDtypeStruct((S, D_MODEL), F32),
        scratch_shapes=[pltpu.VMEM((d, tm // d, LANES), F32) for d in DILATIONS[:-1]],
        compiler_params=_cparams(("parallel",)),
        name="mix_outproj",
    )(x, o_f, o_b, g, y, *attn_args, w_out)


def _mlp_kernel(x_ref, nw_ref, wu_ref, wd_ref, fw_ref, out_ref, *, final):
    acc = x_ref[...]
    h = _rms(acc, nw_ref[...]).astype(BF16)
    for c in range(0, D_FF, FF_CHUNK):
        u = jnp.dot(h, wu_ref[:, c:c + FF_CHUNK], preferred_element_type=F32)
        u = jnp.square(jnp.maximum(u, 0.0)).astype(BF16)
        acc += jnp.dot(u, wd_ref[c:c + FF_CHUNK], preferred_element_type=F32)
    out_ref[...] = _rms(acc, fw_ref[...]) if final else acc


def _mlp(x, norm_w, w_up, w_down, layer, final_w, final):
    S = x.shape[0]
    tm = ROW_TILE
    row = pl.BlockSpec((tm, D_MODEL), lambda i: (i, 0))
    return pl.pallas_call(
        functools.partial(_mlp_kernel, final=final),
        grid=(S // tm,),
        in_specs=[row, _const_spec((1, D_MODEL)), _layer_spec((D_MODEL, D_FF), layer),
                  _layer_spec((D_FF, D_MODEL), layer), _const_spec((1, D_MODEL))],
        out_specs=row,
        out_shape=jax.ShapeDtypeStruct((S, D_MODEL), F32),
        compiler_params=_cparams(("parallel",)),
        name="mlp",
    )(x, norm_w, w_up, w_down, final_w)


def _rope_tables(positions):
    inv_freq = ROPE_THETA ** (-jnp.arange(0, HEAD_DIM, 2, dtype=F32) / HEAD_DIM)
    ang = positions.astype(F32)[:, None] * inv_freq
    cos, sin = jnp.cos(ang), jnp.sin(ang)
    reps = LANES // HEAD_DIM
    cos_t = jnp.tile(cos, (1, 2 * reps))
    sin_t = jnp.tile(jnp.concatenate([-sin, sin], axis=1), (1, reps))
    return cos_t, sin_t


def kernel(x, positions, attn_norm_w, w_in, hgrn_lower_bounds, w_out, mlp_norm_w, w_up, w_down,
           final_norm_w):
    B, S, D = x.shape
    depth = w_in.shape[0]
    assert B == 1 and D == D_MODEL and S % (2 * ATTN_QBLK * max(DILATIONS)) == 0
    assert S % (HGRN_BLOCK * 2) == 0

    p_lb = jax.nn.softmax(hgrn_lower_bounds.astype(F32), axis=0)
    lb_all = jnp.cumsum(p_lb, axis=0) - p_lb[0:1]
    cos_t, sin_t = _rope_tables(positions[0])
    fconsts = _fourier_constants(S)
    w_in16, w_out16 = w_in.astype(BF16), w_out.astype(BF16)
    w_up16, w_down16 = w_up.astype(BF16), w_down.astype(BF16)
    final_w = final_norm_w.reshape(1, D)

    xs = x[0]
    for layer in range(depth):
        (hg, iv, g, fu), qkv = _inproj(xs, attn_norm_w[layer].reshape(1, D), w_in16, layer,
                                       cos_t, sin_t)
        o_f, o_b = _hgrn(hg, iv, lb_all[layer])
        y = _fourier(fu, fconsts)
        attn = [_attn_pattern(q, k, v) for q, k, v in qkv]
        xs = _mix(xs, o_f, o_b, g, y, attn, w_out16, layer)
        xs = _mlp(xs, mlp_norm_w[layer].reshape(1, D), w_up16, w_down16, layer, final_w,
                  final=(layer == depth - 1))
    return xs[None]
```

```python
import functools
import math

import numpy as np
import jax
import jax.numpy as jnp
from jax import lax
from jax.experimental import pallas as pl
from jax.experimental.pallas import tpu as pltpu

F32 = jnp.float32
BF16 = jnp.bfloat16

D_MODEL = 1024
HEAD_DIM = 64
HGRN_WIDTH = 256
FOURIER_WIDTH = 256
FOURIER_GROUPS = 4
ATTN_WIDTH = 512
ATTN_HEADS = 8
HEAD_PAIRS = ATTN_HEADS // 2
LANES = 128
IN_WIDTH = 5 * HGRN_WIDTH + FOURIER_WIDTH + 3 * ATTN_WIDTH
D_FF = 4 * D_MODEL
DILATIONS = (1, 4, 16)
WINDOW_RADIUS = 64
ROPE_THETA = 10000.0
RMS_EPS = 1e-6
NEG_BIG = -1e30
LOG2_E = 1.4426950408889634
LN_2 = 0.6931471805599453

HGRN_CHUNK = 64
HGRN_BLOCK = 512
ATTN_TILE = 4096
ATTN_QBLK = 128
ROW_TILE = 512
FF_CHUNK = 1024
FOURIER_PER_STEP = 16
VMEM_LIMIT = 56 * 1024 * 1024


def _cparams(sem):
    return pltpu.CompilerParams(dimension_semantics=sem, vmem_limit_bytes=VMEM_LIMIT)


def _const_spec(shape):
    nd = len(shape)
    return pl.BlockSpec(shape, lambda *_: (0,) * nd)


def _layer_spec(shape, layer):
    nd = len(shape)
    return pl.BlockSpec((None,) + tuple(shape), lambda *_: (layer,) + (0,) * nd)


def _sigmoid(x):
    return 1.0 / (1.0 + jnp.exp(-x))


def _inproj_kernel(x_ref, nw_ref, w_ref, cos_ref, sin_ref, hg_ref, iv_ref, g_ref, fu_ref, *rest):
    n_pat = len(DILATIONS)
    qkv_refs, stage_refs = rest[:3 * n_pat], rest[3 * n_pat:]
    tm = x_ref.shape[0]

    def store_regrouped(val, which, hp):
        stage_refs[0][0] = val
        qkv_refs[which][hp, 0] = val.astype(BF16)
        for i in range(1, n_pat):
            dp, d = DILATIONS[i - 1], DILATIONS[i]
            ratio = d // dp
            for rp in range(dp):
                for r2 in range(ratio):
                    rows = stage_refs[i - 1][rp, pl.ds(r2, tm // d, stride=ratio), :]
                    r = r2 * dp + rp
                    if i + 1 < n_pat:
                        stage_refs[i][r] = rows
                    qkv_refs[3 * i + which][hp, r] = rows.astype(BF16)

    x = x_ref[...]
    ms = jnp.mean(x * x, axis=-1, keepdims=True)
    h = (x * lax.rsqrt(ms + RMS_EPS) * nw_ref[...]).astype(BF16)

    def proj(c0, c1):
        return jnp.dot(h, w_ref[:, c0:c1], preferred_element_type=F32)

    hg_w = 5 * HGRN_WIDTH
    cos = cos_ref[...]
    sin = sin_ref[...]
    lane = lax.broadcasted_iota(jnp.int32, cos.shape, 1)
    low_half = (lane % HEAD_DIM) < (HEAD_DIM // 2)

    def rotary(t):
        partner = jnp.where(low_half, pltpu.roll(t, LANES - HEAD_DIM // 2, 1),
                            pltpu.roll(t, HEAD_DIM // 2, 1))
        return t * cos + partner * sin

    base = hg_w + FOURIER_WIDTH
    for which in range(3):
        for hp0 in range(0, HEAD_PAIRS, 2):
            c = base + which * ATTN_WIDTH + hp0 * LANES
            both = proj(c, c + 2 * LANES)
            for hp in (hp0, hp0 + 1):
                val = both[:, (hp - hp0) * LANES:(hp - hp0 + 1) * LANES]
                if which == 0:
                    val = rotary(val) * (HEAD_DIM ** -0.5 * LOG2_E)
                elif which == 1:
                    val = rotary(val)
                store_regrouped(val, which, hp)

    W = HGRN_WIDTH
    for j, src in enumerate((0, 2, 3)):
        hg_ref[:, j * W:(j + 1) * W] = proj(src * W, (src + 1) * W)
    iv_ref[...] = proj(W, 2 * W).astype(BF16)
    g_ref[...] = proj(4 * W, 5 * W).astype(BF16)
    fu_ref[...] = proj(hg_w, hg_w + FOURIER_WIDTH)


def _inproj(x, norm_w, w_in, layer, cos_t, sin_t):
    S = x.shape[0]
    tm = ROW_TILE
    row = lambda w: pl.BlockSpec((tm, w), lambda i: (i, 0))
    qkv_specs, qkv_shapes = [], []
    for d in DILATIONS:
        for _ in range(3):
            qkv_specs.append(pl.BlockSpec((HEAD_PAIRS, d, tm // d, LANES), lambda i: (0, 0, i, 0)))
            qkv_shapes.append(jax.ShapeDtypeStruct((HEAD_PAIRS, d, S // d, LANES), BF16))
    outs = pl.pallas_call(
        _inproj_kernel,
        grid=(S // tm,),
        in_specs=[row(D_MODEL), _const_spec((1, D_MODEL)),
                  _layer_spec((D_MODEL, IN_WIDTH), layer),
                  row(LANES), row(LANES)],
        out_specs=[row(3 * HGRN_WIDTH), row(HGRN_WIDTH), row(HGRN_WIDTH), row(FOURIER_WIDTH)]
                  + qkv_specs,
        out_shape=[jax.ShapeDtypeStruct((S, 3 * HGRN_WIDTH), F32),
                   jax.ShapeDtypeStruct((S, HGRN_WIDTH), BF16),
                   jax.ShapeDtypeStruct((S, HGRN_WIDTH), BF16),
                   jax.ShapeDtypeStruct((S, FOURIER_WIDTH), F32)] + qkv_shapes,
        scratch_shapes=[pltpu.VMEM((d, tm // d, LANES), F32) for d in DILATIONS[:-1]],
        compiler_params=_cparams(("parallel",)),
        name="inproj",
    )(x, norm_w, w_in, cos_t, sin_t)
    return outs[:4], [outs[4 + 3 * i:7 + 3 * i] for i in range(len(DILATIONS))]


def _split3(x):
    hi = x.astype(BF16)
    r = x - hi.astype(F32)
    mid = r.astype(BF16)
    lo = (r - mid.astype(F32)).astype(BF16)
    return hi, mid, lo


def _tanh_sigmoid(x):
    return 0.5 * jnp.tanh(0.5 * x) + 0.5


HGRN_SLAB = 2 * HEAD_DIM


def _hgrn_gates(q_in, z, lb, reverse):
    C = q_in.shape[0]
    f = lb + (1.0 - lb) * _sigmoid(z)
    kk = 1.0 - f
    q = q_in * _tanh_sigmoid(q_in)
    r_i = lax.broadcasted_iota(jnp.int32, (C, C), 0)
    c_i = lax.broadcasted_iota(jnp.int32, (C, C), 1)
    causal = (c_i >= r_i) if reverse else (c_i <= r_i)
    ones_tri = causal.astype(BF16)
    hi, mid, lo = _split3(jnp.log(f))
    b = (jnp.dot(ones_tri, hi, preferred_element_type=F32)
         + jnp.dot(ones_tri, mid, preferred_element_type=F32)
         + jnp.dot(ones_tri, lo, preferred_element_type=F32))
    return q, kk, b, causal


HGRN_SAFE_RANGE = 60.0


def _hgrn_decays(q, kk, b, reverse):
    C = q.shape[0]
    edge = 0 if reverse else C - 1
    b_edge = b[edge:edge + 1]
    b_mid = b[C // 2:C // 2 + 1]
    q_loc = q * jnp.exp(b - b_mid)
    k_loc = kk * jnp.exp(b_mid - b)
    q_inter = q * jnp.exp(b)
    k_edge = kk * jnp.exp(b_edge - b)
    spread = jnp.maximum(jnp.abs(b[0:1] - b_mid), jnp.abs(b[C - 1:C] - b_mid))
    return (q_inter.astype(BF16), q_loc.astype(BF16), k_loc.astype(BF16), k_edge.astype(BF16),
            jnp.exp(b_edge), spread)


def _hgrn_local(q_loc, k_loc, k_edge, v16, causal):
    C, W = q_loc.shape
    contract_lanes = (((1,), (1,)), ((), ()))
    lane = lax.broadcasted_iota(jnp.int32, (C, W), 1)
    zero16 = jnp.zeros((C, W), BF16)
    scores = []
    v_rows = []
    for h in range(W // HEAD_DIM):
        in_head = (lane // HEAD_DIM) == h
        a = lax.dot_general(jnp.where(in_head, q_loc, zero16), k_loc, contract_lanes,
                            preferred_element_type=F32)
        scores.append(jnp.where(causal, a, 0.0).astype(BF16))
        v_rows.append(jnp.where(in_head, v16, zero16))
    intra = jnp.dot(jnp.concatenate(scores, axis=1), jnp.concatenate(v_rows, axis=0),
                    preferred_element_type=F32)
    s_r = lax.broadcasted_iota(jnp.int32, (HGRN_SLAB, HGRN_SLAB), 0) // HEAD_DIM
    s_c = lax.broadcasted_iota(jnp.int32, (HGRN_SLAB, HGRN_SLAB), 1) // HEAD_DIM
    upd = []
    for c in range(0, W, HGRN_SLAB):
        u = lax.dot_general(v16[:, c:c + HGRN_SLAB], k_edge[:, c:c + HGRN_SLAB],
                            (((0,), (0,)), ((), ())), preferred_element_type=F32)
        upd.append(jnp.where(s_r == s_c, u, 0.0))
    return intra, upd


def _head_sums(x):
    lane = lax.broadcasted_iota(jnp.int32, (x.shape[0], LANES), 1)
    cols = [x[:, c:c + LANES] for c in range(0, x.shape[1], LANES)]
    step = HEAD_DIM // 2
    while step >= 1:
        low = (lane // step) % 2 == 0
        cols = [xc + jnp.where(low, pltpu.roll(xc, LANES - step, 1), pltpu.roll(xc, step, 1))
                for xc in cols]
        step //= 2
    return jnp.concatenate(cols, axis=1)


def _hgrn_local_pairwise(q, kk, b, v, reverse, row_refs):
    C, W = q.shape
    b_ref, k_ref, v_ref = row_refs
    b_ref[...] = b
    k_ref[...] = kk
    v_ref[...] = v
    t_i = lax.broadcasted_iota(jnp.int32, (C, W), 0)

    def body(s, acc):
        row = pl.ds(s, 1)
        w = q * jnp.exp(jnp.minimum(b - b_ref[row, :], 0.0)) * k_ref[row, :]
        valid = (t_i <= s) if reverse else (t_i >= s)
        return acc + jnp.where(valid, _head_sums(w), 0.0) * v_ref[row, :]

    return lax.fori_loop(0, C, body, jnp.zeros((C, W), F32))


def _hgrn_kernel(qf_ref, vf_ref, zf_ref, qb_ref, vb_ref, zb_ref, lb_ref, of_ref, ob_ref,
                 sf_ref, sb_ref, local_ref, rowb_ref, rowk_ref, rowv_ref):
    @pl.when(pl.program_id(0) == 0)
    def _():
        sf_ref[...] = jnp.zeros_like(sf_ref)
        sb_ref[...] = jnp.zeros_like(sb_ref)

    C = HGRN_CHUNK
    n = HGRN_BLOCK // C
    chains = [((qf_ref, vf_ref, zf_ref, of_ref, sf_ref), lb_ref[0:1], False, list(range(n))),
              ((qb_ref, vb_ref, zb_ref, ob_ref, sb_ref), lb_ref[1:2], True,
               list(range(n - 1, -1, -1)))]
    jobs = []
    for refs, lb, reverse, order in chains:
        for j in order:
            rows = slice(j * C, (j + 1) * C)
            jobs.append(dict(refs=refs, rows=rows, reverse=reverse,
                             gates=_hgrn_gates(refs[0][rows], refs[2][rows], lb, reverse)))
    for job in jobs:
        q, kk, b, causal = job["gates"]
        job["decays"] = _hgrn_decays(q, kk, b, job["reverse"])
        job["causal"] = causal
    for idx, job in enumerate(jobs):
        _, q_loc, k_loc, k_edge, _, _ = job["decays"]
        v16 = job["refs"][1][job["rows"]]
        local_ref[idx], job["upd"] = _hgrn_local(q_loc, k_loc, k_edge, v16, job["causal"])

    spread = functools.reduce(jnp.maximum, [job["decays"][5] for job in jobs])

    @pl.when(jnp.max(spread) > HGRN_SAFE_RANGE)
    def _():
        for idx, job in enumerate(jobs):
            q, kk, b, _ = job["gates"]
            v = job["refs"][1][job["rows"]].astype(F32)
            local_ref[idx] = _hgrn_local_pairwise(q, kk, b, v, job["reverse"],
                                                  (rowb_ref, rowk_ref, rowv_ref))

    contract_lanes = (((1,), (1,)), ((), ()))
    n_slab = HGRN_WIDTH // HGRN_SLAB
    for refs, _, _, _ in chains:
        state = [refs[4][i] for i in range(n_slab)]
        for idx, job in enumerate(jobs):
            if job["refs"] is not refs:
                continue
            q_inter, decay = job["decays"][0], job["decays"][4]
            inter = [lax.dot_general(q_inter[:, i * HGRN_SLAB:(i + 1) * HGRN_SLAB],
                                     state[i].astype(BF16), contract_lanes,
                                     preferred_element_type=F32) for i in range(n_slab)]
            out = jnp.concatenate(inter, axis=1) + local_ref[idx]
            refs[3][job["rows"]] = out.astype(refs[3].dtype)
            state = [state[i] * decay[:, i * HGRN_SLAB:(i + 1) * HGRN_SLAB] + job["upd"][i]
                     for i in range(n_slab)]
        for i, s_i in enumerate(state):
            refs[4][i] = s_i


def _hgrn(hg, iv, lb):
    S = hg.shape[0]
    tb = HGRN_BLOCK
    n = S // tb
    W = HGRN_WIDTH
    fwd = lambda col: pl.BlockSpec((tb, W), lambda i: (i, col))
    bwd = lambda col: pl.BlockSpec((tb, W), lambda i: (n - 1 - i, col))
    state = pltpu.VMEM((W // HGRN_SLAB, HGRN_SLAB, HGRN_SLAB), F32)
    return pl.pallas_call(
        _hgrn_kernel,
        grid=(n,),
        in_specs=[fwd(0), fwd(0), fwd(1), bwd(0), bwd(0), bwd(2), _const_spec((2, W))],
        out_specs=[fwd(0), bwd(0)],
        out_shape=[jax.ShapeDtypeStruct((S, W), BF16)] * 2,
        scratch_shapes=[state, state, pltpu.VMEM((2 * tb // HGRN_CHUNK, HGRN_CHUNK, W), F32)]
                       + [pltpu.VMEM((HGRN_CHUNK, W), F32)] * 3,
        compiler_params=_cparams(("arbitrary",)),
        name="hgrn_scan",
    )(hg, iv, hg, hg, iv, hg, lb)


def _fourier_constants(S):
    n1 = 1 << (int(math.log2(S)) // 2)
    n2 = S // n1
    cg = FOURIER_WIDTH // FOURIER_GROUPS
    j = np.arange(cg)
    ang = 2.0 * np.pi * np.outer(j, j) / cg
    eye = np.eye(FOURIER_GROUPS)
    ch = np.concatenate([np.kron(eye, np.cos(ang)), -np.kron(eye, np.sin(ang))], axis=1)
    a1 = 2.0 * np.pi * np.outer(np.arange(n1), np.arange(n1)) / n1
    fr, fi = np.cos(a1), -np.sin(a1)
    fa = np.block([[fr, -fi], [fi, fr]])
    a2 = 2.0 * np.pi * np.outer(np.arange(n2), np.arange(n2)) / n2
    fb = np.concatenate([np.cos(a2), np.sin(a2)], axis=1)
    at = 2.0 * np.pi * np.outer(np.arange(n2), np.arange(n1)) / S
    twr = np.repeat(np.cos(at)[:, :, None], LANES, axis=2)
    twi = np.repeat(-np.sin(at)[:, :, None], LANES, axis=2)
    return (n1, n2, jnp.asarray(ch, F32), jnp.asarray(fa, F32), jnp.asarray(fb, F32),
            jnp.asarray(twr, F32), jnp.asarray(twi, F32))


def _fourier_a_kernel(u_ref, ch_ref, fa_ref, twr_ref, twi_ref, z_ref, u2_ref, z2_ref):
    n1, slabs, W = u_ref.shape
    cols = W // LANES
    ch = ch_ref[...].astype(BF16)
    fa = fa_ref[...].astype(BF16)
    u_blk = u_ref[...]
    for c in range(cols):
        u2_ref[c] = u_blk[:, :, c * LANES:(c + 1) * LANES].reshape(n1 * slabs, LANES)
    for s in range(slabs):
        rows = pl.ds(s, n1, stride=slabs)
        x = jnp.concatenate([u2_ref[c, rows, :] for c in range(cols)], axis=1).astype(BF16)
        uc = jnp.dot(x, ch, preferred_element_type=F32)
        stacked = jnp.concatenate([uc[:, :W], uc[:, W:]], axis=0).astype(BF16)
        a = jnp.dot(fa, stacked, preferred_element_type=F32)
        ar, ai = a[:n1], a[n1:]
        twr = jnp.concatenate([twr_ref[s]] * cols, axis=1)
        twi = jnp.concatenate([twi_ref[s]] * cols, axis=1)
        for part, val in enumerate((ar * twr - ai * twi, ar * twi + ai * twr)):
            for c in range(cols):
                z2_ref[part, c, rows, :] = val[:, c * LANES:(c + 1) * LANES]
    for part in range(2):
        z_ref[part] = jnp.concatenate(
            [z2_ref[part, c].reshape(n1, slabs, LANES) for c in range(cols)],
            axis=2).astype(z_ref.dtype)


def _fourier_b_kernel(z_ref, fb_ref, y_ref, y2_ref, *, scale):
    n2, count, W = y_ref.shape
    cols = W // LANES
    fb = fb_ref[...].astype(BF16)
    for p in range(count):
        rhs = jnp.concatenate([z_ref[0, p], z_ref[1, p]], axis=0)
        y = jnp.dot(fb, rhs, preferred_element_type=F32) * scale
        for c in range(cols):
            y2_ref[c, pl.ds(p, n2, stride=count), :] = y[:, c * LANES:(c + 1) * LANES]
    y_ref[...] = jnp.concatenate(
        [y2_ref[c].reshape(n2, count, LANES) for c in range(cols)], axis=2).astype(y_ref.dtype)


def _fourier(u, consts):
    S, W = u.shape
    n1, n2, ch, fa, fb, twr, twi = consts
    per = FOURIER_PER_STEP
    z = pl.pallas_call(
        _fourier_a_kernel,
        grid=(n2 // per,),
        in_specs=[pl.BlockSpec((n1, per, W), lambda i: (0, i, 0)), _const_spec(ch.shape),
                  _const_spec(fa.shape), pl.BlockSpec((per, n1, LANES), lambda i: (i, 0, 0)),
                  pl.BlockSpec((per, n1, LANES), lambda i: (i, 0, 0))],
        out_specs=pl.BlockSpec((2, n1, per, W), lambda i: (0, 0, i, 0)),
        out_shape=jax.ShapeDtypeStruct((2, n1, n2, W), BF16),
        scratch_shapes=[pltpu.VMEM((W // LANES, n1 * per, LANES), F32),
                        pltpu.VMEM((2, W // LANES, n1 * per, LANES), F32)],
        compiler_params=_cparams(("parallel",)),
        name="fourier_a",
    )(u.reshape(n1, n2, W), ch, fa, twr, twi)
    cg = W // FOURIER_GROUPS
    y = pl.pallas_call(
        functools.partial(_fourier_b_kernel, scale=1.0 / math.sqrt(S * cg)),
        grid=(n1 // per,),
        in_specs=[pl.BlockSpec((2, per, n2, W), lambda i: (0, i, 0, 0)), _const_spec(fb.shape)],
        out_specs=pl.BlockSpec((n2, per, W), lambda i: (0, i, 0)),
        out_shape=jax.ShapeDtypeStruct((n2, n1, W), BF16),
        scratch_shapes=[pltpu.VMEM((W // LANES, n2 * per, LANES), F32)],
        compiler_params=_cparams(("parallel",)),
        name="fourier_b",
    )(z, fb)
    return y.reshape(S, W)


LSE_GROUP = HEAD_DIM // HEAD_PAIRS


def _attn_kernel(q_ref, k_ref, kp_ref, kn_ref, v_ref, vp_ref, vn_ref, o_ref, lse_ref,
                 kext_ref, vaug_ref):
    R = WINDOW_RADIUS
    n_pair, rows = q_ref.shape[0], q_ref.shape[1]
    QB = ATTN_QBLK
    KW = QB + 2 * R
    t = pl.program_id(1)
    nt = pl.num_programs(1)

    lane = lax.broadcasted_iota(jnp.int32, (QB, LANES), 1)
    head0 = lane < HEAD_DIM
    pair_of_lane = (lane % HEAD_DIM) // LSE_GROUP
    r_i = lax.broadcasted_iota(jnp.int32, (QB, KW), 0)
    c_i = lax.broadcasted_iota(jnp.int32, (QB, KW), 1)
    band = jnp.where((c_i >= r_i) & (c_i <= r_i + 2 * R), 0.0, NEG_BIG)
    first_bias = band + jnp.where((t == 0) & (c_i < R), NEG_BIG, 0.0)
    last_bias = band + jnp.where((t == nt - 1) & (c_i >= QB + R), NEG_BIG, 0.0)
    zero16 = jnp.zeros((QB, LANES), BF16)
    contract_lanes = (((1,), (1,)), ((), ()))
    nblk = rows // QB

    for b in range(n_pair):
        kext_ref[b, 0:R] = kp_ref[b]
        kext_ref[b, R:R + rows] = k_ref[b]
        kext_ref[b, R + rows:] = kn_ref[b]
        for h in range(2):
            for lo, hi, src in ((0, R, vp_ref), (R, R + rows, v_ref),
                                (R + rows, rows + 2 * R, vn_ref)):
                val = src[b]
                in_head0 = lax.broadcasted_iota(jnp.int32, val.shape, 1) < HEAD_DIM
                mine = in_head0 if h == 0 else ~in_head0
                vaug_ref[b, h, lo:hi, 0:LANES] = jnp.where(mine, val, jnp.zeros_like(val))
                vaug_ref[b, h, lo:hi, LANES:] = mine.astype(BF16)

    for jb in range(nblk):
        bias = first_bias if jb == 0 else (last_bias if jb == nblk - 1 else band)
        packed = None
        for b in range(n_pair):
            q = q_ref[b, jb * QB:(jb + 1) * QB]
            kwin = kext_ref[b, jb * QB:jb * QB + KW]
            q2 = jnp.concatenate([jnp.where(head0, q, zero16), jnp.where(head0, zero16, q)],
                                 axis=0)
            s2 = lax.dot_general(q2, kwin, contract_lanes, preferred_element_type=F32)
            tot = None
            maxes = []
            for h in range(2):
                s = s2[h * QB:(h + 1) * QB] + bias
                m = jnp.max(s, axis=-1, keepdims=True)
                p = jnp.exp2(s - m).astype(BF16)
                part = jnp.dot(p, vaug_ref[b, h, jb * QB:jb * QB + KW],
                               preferred_element_type=F32)
                tot = part if tot is None else tot + part
                maxes.append(m)
            l = tot[:, LANES:]
            o_ref[b, jb * QB:(jb + 1) * QB] = (tot[:, :LANES] / l).astype(BF16)
            lse = jnp.where(head0, maxes[0], maxes[1]) * LN_2 + jnp.log(l)
            packed = lse if packed is None else jnp.where(pair_of_lane == b, lse, packed)
        lse_ref[jb * QB:(jb + 1) * QB] = packed


def _attn_pattern(q, k, v):
    P, d, L, _ = q.shape
    rows = min(ATTN_TILE // P, L)
    assert L % rows == 0 and rows % ATTN_QBLK == 0 and rows // ATTN_QBLK >= 2
    R = WINDOW_RADIUS
    halo_per_tile = rows // R
    n_halo = L // R
    main = pl.BlockSpec((P, None, rows, LANES), lambda r, t: (0, r, t, 0))
    prev = pl.BlockSpec((P, None, R, LANES),
                        lambda r, t: (0, r, jnp.maximum(t * halo_per_tile - 1, 0), 0))
    nxt = pl.BlockSpec((P, None, R, LANES),
                       lambda r, t: (0, r, jnp.minimum((t + 1) * halo_per_tile, n_halo - 1), 0))
    return pl.pallas_call(
        _attn_kernel,
        grid=(d, L // rows),
        in_specs=[main, main, prev, nxt, main, prev, nxt],
        out_specs=[main, pl.BlockSpec((None, rows, LANES), lambda r, t: (r, t, 0))],
        out_shape=[jax.ShapeDtypeStruct((P, d, L, LANES), BF16),
                   jax.ShapeDtypeStruct((d, L, LANES), F32)],
        scratch_shapes=[pltpu.VMEM((P, rows + 2 * R, LANES), BF16),
                        pltpu.VMEM((P, 2, rows + 2 * R, 2 * LANES), BF16)],
        compiler_params=_cparams(("parallel", "parallel")),
        name=f"dilated_attn_d{d}",
    )(q, k, k, k, v, v, v)


def _rms(x, w):
    ms = jnp.mean(x * x, axis=-1, keepdims=True)
    return x * lax.rsqrt(ms + RMS_EPS) * w


def _mix_kernel(x_ref, of_ref, ob_ref, g_ref, y_ref, *rest):
    n_attn = 2 * len(DILATIONS)
    attn_refs, wo_ref, out_ref = rest[:n_attn], rest[n_attn], rest[n_attn + 1]
    stage_refs = rest[n_attn + 2:]
    tm = x_ref.shape[0]
    o = of_ref[...].astype(F32) + ob_ref[...].astype(F32)
    W = o.shape[1]
    r_h = lax.broadcasted_iota(jnp.int32, (W, W), 0) // HEAD_DIM
    c_h = lax.broadcasted_iota(jnp.int32, (W, W), 1) // HEAD_DIM
    head_sum = (r_h == c_h).astype(BF16)
    sq_hi, sq_mid, _ = _split3(o * o)
    ms = (jnp.dot(sq_hi, head_sum, preferred_element_type=F32)
          + jnp.dot(sq_mid, head_sum, preferred_element_type=F32)) * (1.0 / HEAD_DIM)
    g = g_ref[...].astype(F32)
    o_a = o * lax.rsqrt(ms + RMS_EPS) * (g * _sigmoid(g))

    acc = x_ref[...]
    acc += jnp.dot(o_a.astype(BF16), wo_ref[0:W], preferred_element_type=F32)
    acc += jnp.dot(y_ref[...], wo_ref[W:2 * W], preferred_element_type=F32)

    def natural(slabs, i):
        cur = [sl.astype(F32) for sl in slabs]
        for lvl in range(i, 0, -1):
            dp, d = DILATIONS[lvl - 1], DILATIONS[lvl]
            ratio = d // dp
            for rp in range(dp):
                for r2 in range(ratio):
                    stage_refs[lvl - 1][rp, pl.ds(r2, tm // d, stride=ratio), :] = cur[r2 * dp + rp]
            cur = [stage_refs[lvl - 1][rp] for rp in range(dp)]
        return cur[0]

    n_pat = len(DILATIONS)
    lses = [natural([attn_refs[2 * i + 1][r] for r in range(DILATIONS[i])], i)
            for i in range(n_pat)]
    m = functools.reduce(jnp.maximum, lses)
    ws = [jnp.exp(l - m) for l in lses]
    inv = 1.0 / sum(ws)
    src = lax.broadcasted_iota(jnp.int32, (LANES, HEAD_PAIRS * LANES), 0)
    dst = lax.broadcasted_iota(jnp.int32, (LANES, HEAD_PAIRS * LANES), 1)
    dst_lane = dst % LANES
    wanted = (dst_lane // HEAD_DIM) * HEAD_DIM + (dst // LANES) * LSE_GROUP + dst_lane % LSE_GROUP
    spread = (src == wanted).astype(BF16)
    spread2 = jnp.concatenate([spread, spread], axis=0)
    weights = []
    for w in ws:
        wn = w * inv
        hi = wn.astype(BF16)
        lo = (wn - hi.astype(F32)).astype(BF16)
        weights.append(jnp.dot(jnp.concatenate([hi, lo], axis=1), spread2,
                               preferred_element_type=F32))
    mixed = []
    for hp in range(HEAD_PAIRS):
        o_c = sum(weights[i][:, hp * LANES:(hp + 1) * LANES]
                  * natural([attn_refs[2 * i][hp, r] for r in range(DILATIONS[i])], i)
                  for i in range(n_pat))
        mixed.append(o_c.astype(BF16))
    acc += jnp.dot(jnp.concatenate(mixed, axis=1), wo_ref[2 * W:], preferred_element_type=F32)
    out_ref[...] = acc


def _mix(x, o_f, o_b, g, y, attn, w_out, layer):
    S = x.shape[0]
    tm = ROW_TILE
    row = lambda w, col=0: pl.BlockSpec((tm, w), lambda i: (i, col))
    attn_specs, attn_args = [], []
    for d, (o, lse) in zip(DILATIONS, attn):
        attn_specs += [pl.BlockSpec((HEAD_PAIRS, d, tm // d, LANES), lambda i: (0, 0, i, 0)),
                       pl.BlockSpec((d, tm // d, LANES), lambda i: (0, i, 0))]
        attn_args += [o, lse]
    return pl.pallas_call(
        _mix_kernel,
        grid=(S // tm,),
        in_specs=[row(D_MODEL), row(HGRN_WIDTH), row(HGRN_WIDTH), row(HGRN_WIDTH),
                  row(FOURIER_WIDTH)] + attn_specs
                 + [_layer_spec((D_MODEL, D_MODEL), layer)],
        out_specs=row(D_MODEL),
        out_shape=jax.ShapeDtypeStruct((S, D_MODEL), F32),
        scratch_shapes=[pltpu.VMEM((d, tm // d, LANES), F32) for d in DILATIONS[:-1]],
        compiler_params=_cparams(("parallel",)),
        name="mix_outproj",
    )(x, o_f, o_b, g, y, *attn_args, w_out)


def _mlp_kernel(x_ref, nw_ref, wu_ref, wd_ref, fw_ref, out_ref, *, final):
    acc = x_ref[...]
    h = _rms(acc, nw_ref[...]).astype(BF16)
    for c in range(0, D_FF, FF_CHUNK):
        u = jnp.dot(h, wu_ref[:, c:c + FF_CHUNK], preferred_element_type=F32)
        u = jnp.square(jnp.maximum(u, 0.0)).astype(BF16)
        acc += jnp.dot(u, wd_ref[c:c + FF_CHUNK], preferred_element_type=F32)
    out_ref[...] = _rms(acc, fw_ref[...]) if final else acc


def _mlp(x, norm_w, w_up, w_down, layer, final_w, final):
    S = x.shape[0]
    tm = ROW_TILE
    row = pl.BlockSpec((tm, D_MODEL), lambda i: (i, 0))
    return pl.pallas_call(
        functools.partial(_mlp_kernel, final=final),
        grid=(S // tm,),
        in_specs=[row, _const_spec((1, D_MODEL)), _layer_spec((D_MODEL, D_FF), layer),
                  _layer_spec((D_FF, D_MODEL), layer), _const_spec((1, D_MODEL))],
        out_specs=row,
        out_shape=jax.ShapeDtypeStruct((S, D_MODEL), F32),
        compiler_params=_cparams(("parallel",)),
        name="mlp",
    )(x, norm_w, w_up, w_down, final_w)


def _rope_tables(positions):
    inv_freq = ROPE_THETA ** (-jnp.arange(0, HEAD_DIM, 2, dtype=F32) / HEAD_DIM)
    ang = positions.astype(F32)[:, None] * inv_freq
    cos, sin = jnp.cos(ang), jnp.sin(ang)
    reps = LANES // HEAD_DIM
    cos_t = jnp.tile(cos, (1, 2 * reps))
    sin_t = jnp.tile(jnp.concatenate([-sin, sin], axis=1), (1, reps))
    return cos_t, sin_t


def kernel(x, positions, attn_norm_w, w_in, hgrn_lower_bounds, w_out, mlp_norm_w, w_up, w_down,
           final_norm_w):
    B, S, D = x.shape
    depth = w_in.shape[0]
    assert B == 1 and D == D_MODEL and S % (2 * ATTN_QBLK * max(DILATIONS)) == 0
    assert S % (HGRN_BLOCK * 2) == 0

    p_lb = jax.nn.softmax(hgrn_lower_bounds.astype(F32), axis=0)
    lb_all = jnp.cumsum(p_lb, axis=0) - p_lb[0:1]
    cos_t, sin_t = _rope_tables(positions[0])
    fconsts = _fourier_constants(S)
    w_in16, w_out16 = w_in.astype(BF16), w_out.astype(BF16)
    w_up16, w_down16 = w_up.astype(BF16), w_down.astype(BF16)
    final_w = final_norm_w.reshape(1, D)

    xs = x[0]
    for layer in range(depth):
        (hg, iv, g, fu), qkv = _inproj(xs, attn_norm_w[layer].reshape(1, D), w_in16, layer,
                                       cos_t, sin_t)
        o_f, o_b = _hgrn(hg, iv, lb_all[layer])
        y = _fourier(fu, fconsts)
        attn = [_attn_pattern(q, k, v) for q, k, v in qkv]
        xs = _mix(xs, o_f, o_b, g, y, attn, w_out16, layer)
        xs = _mlp(xs, mlp_norm_w[layer].reshape(1, D), w_up16, w_down16, layer, final_w,
                  final=(layer == depth - 1))
    return xs[None]
```
